```python
import jax
import jax.numpy as jnp
from jax import lax
import numpy as np

D_MODEL = 2048
BATCH = 8
SEQ = 4096
DEPTH = 4

GRID_W = 64
CTX_LEN = 256
N_EVEN = (DEPTH + 1) // 2
N_ODD = DEPTH // 2
EPS = 1e-6
F_FLOOR = 1e-30
N_MOD = 9

D_FF = ((8 * D_MODEL // 3 + 255) // 256) * 256

MIX_W = D_MODEL
GLA_HEADS = 4
GLA_W = MIX_W // 2
GLA_DV = GLA_W // GLA_HEADS
GLA_DK = GLA_DV // 2
GLA_QK = GLA_HEADS * GLA_DK
GLA_GATE_RANK = 16
GLA_GATE_TAU = 16.0
HGRN_EXPAND = 128
HGRN_W = MIX_W // 2
HGRN_HEADS = HGRN_W // HGRN_EXPAND
HGRN_DV = HGRN_W // HGRN_HEADS
CHUNK = 64
ATTN_HEAD_DIM = 128
ATTN_HEADS = D_MODEL // ATTN_HEAD_DIM
ATTN_KV_HEADS = ATTN_HEADS // 4
ATTN_GROUP = ATTN_HEADS // ATTN_KV_HEADS
ROPE_THETA = 10000.0
Q_BLOCK = 128

EVEN_SPLIT = (GLA_QK, GLA_QK, GLA_W, GLA_W, 2 * GLA_GATE_RANK, HGRN_W, 2 * HGRN_W, HGRN_W, HGRN_W)
ODD_SPLIT = (ATTN_HEADS * ATTN_HEAD_DIM, ATTN_KV_HEADS * ATTN_HEAD_DIM, ATTN_KV_HEADS * ATTN_HEAD_DIM)

kernel_name = "hybrid_gla_hgrn2_gqa_macaron_dit"


def split_cols(a, sizes):
    return jnp.split(a, [int(i) for i in np.cumsum(sizes)[:-1]], axis=-1)


def rms_norm(x):
    xf = x.astype(jnp.float32)
    return (xf * lax.rsqrt(jnp.mean(xf * xf, axis=-1, keepdims=True) + EPS)).astype(x.dtype)


def modulate(x, shift, scale):
    return rms_norm(x) * (1 + scale) + shift


def swiglu(h, w_gate, w_up, w_down):
    return (jax.nn.silu(h @ w_gate) * (h @ w_up)) @ w_down


def half_step_ffn(x, shift, scale, gate, w_gate, w_up, w_down):
    return x + 0.5 * gate * swiglu(modulate(x, shift, scale), w_gate, w_up, w_down)


def to_heads(a, n_heads):
    b, l, _ = a.shape
    return a.reshape(b, l, n_heads, -1).transpose(0, 2, 1, 3)


def from_heads(a):
    b, n, l, d = a.shape
    return a.transpose(0, 2, 1, 3).reshape(b, l, n * d)


def chunk_gated_scan(q, k, v, log_g, s0):
    b, h, t, _ = q.shape
    dv = v.shape[-1]
    n = t // CHUNK

    def to_chunks(a):
        return jnp.moveaxis(a.reshape(b, h, n, CHUNK, a.shape[-1]), 2, 0)

    causal = jnp.tril(jnp.ones((CHUNK, CHUNK), bool))[:, :, None]

    def step(s, inp):
        qi, ki, vi, gi = (a.astype(jnp.float32) for a in inp)
        cum = jnp.cumsum(gi, axis=2)
        o_inter = jnp.einsum('bhtk,bhkv->bhtv', qi * jnp.exp(cum), s)
        rel = cum[:, :, :, None, :] - cum[:, :, None, :, :]
        decay = jnp.where(causal, jnp.exp(jnp.where(causal, rel, 0.0)), 0.0)
        attn = jnp.einsum('bhtsk,bhsk->bhts', qi[:, :, :, None, :] * decay, ki)
        o = o_inter + jnp.einsum('bhts,bhsv->bhtv', attn, vi)
        last = cum[:, :, -1]
        s_new = s * jnp.exp(last)[..., None] + jnp.einsum(
            'bhsk,bhsv->bhkv', ki * jnp.exp(last[:, :, None, :] - cum), vi)
        return s_new, o.astype(v.dtype)

    _, oc = lax.scan(step, s0, tuple(map(to_chunks, (q, k, v, log_g))))
    return jnp.moveaxis(oc, 0, 2).reshape(b, h, t, dv)


def segment_reverse(a, ctx_len):
    return jnp.concatenate([jnp.flip(a[:, :, :ctx_len], axis=2), jnp.flip(a[:, :, ctx_len:], axis=2)], axis=2)


def bidirectional_recurrence(q, v, k_f, g_f, k_b, g_b, ctx_len):
    b, h, _, dk = q.shape
    s0 = jnp.zeros((b, h, dk, v.shape[-1]), jnp.float32)
    o_f = chunk_gated_scan(q, k_f, v, g_f, s0)
    r = lambda a: segment_reverse(a, ctx_len)
    o_b = chunk_gated_scan(r(q), r(k_b), r(v), r(g_b), s0)
    return o_f + r(o_b)


def gla_hgrn_mixer(h, ctx_len, need_ctx, w_in, gla_w2, gla_b, gla_g, hgrn_lb, hgrn_g, w_out):
    bsz, L, _ = h.shape
    gq, gk, gv, gr, g_lr, hq, hf, hi, hg = split_cols(h @ w_in, EVEN_SPLIT)
    q = to_heads(gq, GLA_HEADS) * (GLA_DK ** -0.5)
    k = to_heads(gk, GLA_HEADS)
    v = to_heads(gv, GLA_HEADS)
    gate_logit = jnp.einsum('bldr,drk->dblk', g_lr.reshape(bsz, L, 2, GLA_GATE_RANK), gla_w2) + gla_b[:, None, None, :]
    log_a = jax.nn.log_sigmoid(gate_logit.astype(jnp.float32)) / GLA_GATE_TAU
    o_a = bidirectional_recurrence(q, v, k, to_heads(log_a[0], GLA_HEADS),
                                   k, to_heads(log_a[1], GLA_HEADS), ctx_len)
    y_a = from_heads(rms_norm(o_a) * gla_g) * jax.nn.silu(gr)
    q_h = jax.nn.silu(to_heads(hq, HGRN_HEADS))
    i_h = to_heads(hi, HGRN_HEADS)
    lb = hgrn_lb.astype(jnp.float32)
    f_logit = hf.reshape(bsz, L, 2, HGRN_W).astype(jnp.float32)
    f = lb + (1.0 - lb) * jax.nn.sigmoid(f_logit)
    log_f = jnp.log(jnp.maximum(f, F_FLOOR))
    k_in = ((1.0 - lb) * jax.nn.sigmoid(-f_logit)).astype(h.dtype)
    o_b = bidirectional_recurrence(
        q_h, i_h,
        to_heads(k_in[:, :, 0], HGRN_HEADS), to_heads(log_f[:, :, 0], HGRN_HEADS),
        to_heads(k_in[:, :, 1], HGRN_HEADS), to_heads(log_f[:, :, 1], HGRN_HEADS), ctx_len)
    y_b = from_heads(rms_norm(o_b) * hgrn_g) * jax.nn.silu(hg)
    o = jnp.concatenate([y_a, y_b], axis=-1)
    if not need_ctx:
        o = o[:, ctx_len:]
    return o @ w_out


def axial_rope_tables(rows):
    row = jnp.repeat(jnp.arange(rows), GRID_W)
    col = jnp.tile(jnp.arange(GRID_W), rows)
    half = ATTN_HEAD_DIM // 2
    inv_freq = ROPE_THETA ** (-jnp.arange(0, half, 2, dtype=jnp.float32) / half)
    pos = jnp.stack([row, col], axis=-1).astype(jnp.float32)
    ang = pos[:, :, None] * inv_freq
    ang = jnp.concatenate([ang, ang], axis=-1)
    return jnp.cos(ang), jnp.sin(ang)


def axial_rope(x, cos, sin):
    b, s, h, d = x.shape
    xf = x.astype(jnp.float32).reshape(b, s, h, 2, d // 2)
    x1, x2 = jnp.split(xf, 2, axis=-1)
    rot = jnp.concatenate([-x2, x1], axis=-1)
    out = xf * cos[None, :, None] + rot * sin[None, :, None]
    return out.reshape(b, s, h, d).astype(x.dtype)


def attend(q, k, v):
    s = jnp.einsum('bqhgd,bkhd->bhgqk', q, k).astype(jnp.float32)
    p = jax.nn.softmax(s, axis=-1).astype(v.dtype)
    return jnp.einsum('bhgqk,bkhd->bqhgd', p, v)


def gqa_mixer(h, ctx_len, need_ctx, w_in, q_g, k_g, w_out, cos, sin):
    bsz, L, _ = h.shape
    n_lat = L - ctx_len
    q, k, v = split_cols(h @ w_in, ODD_SPLIT)
    q = rms_norm(q.reshape(bsz, L, ATTN_HEADS, ATTN_HEAD_DIM)) * q_g * (ATTN_HEAD_DIM ** -0.5)
    k = rms_norm(k.reshape(bsz, L, ATTN_KV_HEADS, ATTN_HEAD_DIM)) * k_g
    v = v.reshape(bsz, L, ATTN_KV_HEADS, ATTN_HEAD_DIM)
    q_c, q_l = q[:, :ctx_len], axial_rope(q[:, ctx_len:], cos, sin)
    k_c, k_l = k[:, :ctx_len], axial_rope(k[:, ctx_len:], cos, sin)
    v_c, v_l = v[:, :ctx_len], v[:, ctx_len:]
    keys = jnp.concatenate([k_l, k_c], axis=1)
    vals = jnp.concatenate([v_l, v_c], axis=1)
    qb = q_l.reshape(bsz, n_lat // Q_BLOCK, Q_BLOCK, ATTN_KV_HEADS, ATTN_GROUP, ATTN_HEAD_DIM)
    o_l = lax.map(lambda qi: attend(qi, keys, vals), jnp.moveaxis(qb, 1, 0))
    o = jnp.moveaxis(o_l, 0, 1).reshape(bsz, n_lat, ATTN_HEADS * ATTN_HEAD_DIM)
    if need_ctx:
        o_c = attend(q_c.reshape(bsz, ctx_len, ATTN_KV_HEADS, ATTN_GROUP, ATTN_HEAD_DIM), k_c, v_c)
        o = jnp.concatenate([o_c.reshape(bsz, ctx_len, -1), o], axis=1)
    return o @ w_out


def setup_inputs(seed: int = 0) -> dict:
    key = jax.random.key(seed)
    ks = jax.random.split(key, 22)
    nrm = lambda k, shape, scale: jax.random.normal(k, shape, jnp.float32) * scale
    D = D_MODEL
    return {
        "x": nrm(ks[0], (BATCH, SEQ, D), 1.0),
        "c": nrm(ks[1], (BATCH, D), 1.0),
        "ctx": nrm(ks[2], (BATCH, CTX_LEN, D), 1.0),
        "c_ctx": nrm(ks[3], (D,), 1.0),
        "w_mod": nrm(ks[4], (DEPTH, D, N_MOD * D), 0.5 * D ** -0.5),
        "b_mod": nrm(ks[5], (DEPTH, N_MOD * D), 0.01),
        "ffn_w_gate": nrm(ks[6], (DEPTH, 2, D, D_FF), D ** -0.5),
        "ffn_w_up": nrm(ks[7], (DEPTH, 2, D, D_FF), D ** -0.5),
        "ffn_w_down": nrm(ks[8], (DEPTH, 2, D_FF, D), D_FF ** -0.5),
        "mix_ab_w_in": nrm(ks[9], (N_EVEN, D, sum(EVEN_SPLIT)), D ** -0.5),
        "gla_gate_w2": nrm(ks[10], (N_EVEN, 2, GLA_GATE_RANK, GLA_QK), GLA_GATE_RANK ** -0.5),
        "gla_gate_b": nrm(ks[11], (N_EVEN, 2, GLA_QK), 0.01),
        "gla_norm_g": 1.0 + nrm(ks[12], (N_EVEN, GLA_DV), 0.02),
        "hgrn_lb_logits": nrm(ks[13], (2, N_EVEN, HGRN_W), 0.5),
        "hgrn_norm_g": 1.0 + nrm(ks[14], (N_EVEN, HGRN_DV), 0.02),
        "mix_ab_w_out": nrm(ks[15], (N_EVEN, MIX_W, D), MIX_W ** -0.5),
        "attn_w_in": nrm(ks[16], (N_ODD, D, sum(ODD_SPLIT)), D ** -0.5),
        "attn_q_norm_g": 1.0 + nrm(ks[17], (N_ODD, ATTN_HEAD_DIM), 0.02),
        "attn_k_norm_g": 1.0 + nrm(ks[18], (N_ODD, ATTN_HEAD_DIM), 0.02),
        "attn_w_out": nrm(ks[19], (N_ODD, ATTN_HEADS * ATTN_HEAD_DIM, D), (ATTN_HEADS * ATTN_HEAD_DIM) ** -0.5),
        "final_norm_g": 1.0 + nrm(ks[20], (D,), 0.02),
    }


def reference(x, c, ctx, c_ctx, w_mod, b_mod, ffn_w_gate, ffn_w_up, ffn_w_down,
              mix_ab_w_in, gla_gate_w2, gla_gate_b, gla_norm_g, hgrn_lb_logits, hgrn_norm_g, mix_ab_w_out,
              attn_w_in, attn_q_norm_g, attn_k_norm_g, attn_w_out, final_norm_g):
    n_lat = x.shape[1]
    ctx_len = ctx.shape[1]
    rows = n_lat // GRID_W
    cos, sin = axial_rope_tables(rows)
    p_lb = jax.nn.softmax(hgrn_lb_logits.astype(jnp.float32), axis=1)
    lower_bounds = jnp.cumsum(p_lb, axis=1) - p_lb[:, :1]
    silu_c = jax.nn.silu(c)
    silu_cc = jax.nn.silu(c_ctx)
    xl, xc = x, ctx
    for layer in range(DEPTH):
        last = layer == DEPTH - 1
        j = layer // 2
        sh1, sc1, g1, sh2, sc2, g2, sh3, sc3, g3 = jnp.split(
            (silu_c @ w_mod[layer] + b_mod[layer])[:, None, :], N_MOD, axis=-1)
        csh1, csc1, cg1, csh2, csc2, cg2, csh3, csc3, cg3 = jnp.split(
            silu_cc @ w_mod[layer] + b_mod[layer], N_MOD, axis=-1)
        w_pre = (ffn_w_gate[layer, 0], ffn_w_up[layer, 0], ffn_w_down[layer, 0])
        xl = half_step_ffn(xl, sh1, sc1, g1, *w_pre)
        xc = half_step_ffn(xc, csh1, csc1, cg1, *w_pre)
        h = jnp.concatenate([modulate(xc, csh2, csc2), modulate(xl, sh2, sc2)], axis=1)
        if layer % 2 == 0:
            y = gla_hgrn_mixer(h, ctx_len, not last, mix_ab_w_in[j], gla_gate_w2[j], gla_gate_b[j],
                               gla_norm_g[j], lower_bounds[:, j], hgrn_norm_g[j], mix_ab_w_out[j])
        else:
            y = gqa_mixer(h, ctx_len, not last, attn_w_in[j], attn_q_norm_g[j], attn_k_norm_g[j],
                          attn_w_out[j], cos, sin)
        if last:
            xl = xl + g2 * y
        else:
            xl = xl + g2 * y[:, ctx_len:]
            xc = xc + cg2 * y[:, :ctx_len]
        w_post = (ffn_w_gate[layer, 1], ffn_w_up[layer, 1], ffn_w_down[layer, 1])
        xl = half_step_ffn(xl, sh3, sc3, g3, *w_post)
        if not last:
            xc = half_step_ffn(xc, csh3, csc3, cg3, *w_post)
    return rms_norm(xl) * final_norm_g
```

```python
import functools

import jax
import jax.numpy as jnp
import numpy as np
from jax import lax
from jax.experimental import pallas as pl
from jax.experimental.pallas import tpu as pltpu

F32 = jnp.float32
MXU_DTYPE = jnp.bfloat16

EPS = 1e-6
F_FLOOR = 1e-30
N_MOD = 9
GRID_W = 64
GLA_HEADS = 4
GLA_GATE_RANK = 16
GLA_GATE_TAU = 16.0
HGRN_EXPAND = 128
ATTN_HEAD_DIM = 128
ATTN_GROUP = 4
ROPE_THETA = 10000.0

LANES = 128
MXU_N = 256
VMEM_LIMIT_BYTES = 56 * 1024 * 1024

SCAN_CHUNK = 64
BF16_SUBLANES = 16
SCAN_WHOLE_ROWS = BF16_SUBLANES
MOD_ROWS = 16


def _cparams(*sem):
    return pltpu.CompilerParams(dimension_semantics=sem, vmem_limit_bytes=VMEM_LIMIT_BYTES)


def _sigmoid(x):
    return 1.0 / (1.0 + jnp.exp(-x))


def _silu(x):
    return x * _sigmoid(x)


def _rms(x):
    return x * lax.rsqrt(jnp.mean(x * x, axis=-1, keepdims=True) + EPS)


def _dot(a, b):
    return jnp.dot(a, b, preferred_element_type=F32)


def _dot_nt(a, b):
    return lax.dot_general(a, b, (((1,), (1,)), ((), ())), preferred_element_type=F32)


def _dot_tn(a, b):
    return lax.dot_general(a, b, (((0,), (0,)), ((), ())), preferred_element_type=F32)


def _row_tile(n_lat, n_ctx_total, cap):
    t = cap
    while n_lat % t or n_ctx_total % t:
        t //= 2
    return t


def _mod_kernel(c_ref, w_ref, b_ref, o_ref):
    s = _silu(c_ref[...]).astype(MXU_DTYPE)
    o_ref[...] = _dot(s, w_ref[...].astype(MXU_DTYPE)) + b_ref[...]


def _modulation(cond, w_mod, b_mod):
    depth, d, n = w_mod.shape
    tn = 1024
    return pl.pallas_call(
        _mod_kernel,
        grid=(depth, n // tn),
        in_specs=[pl.BlockSpec((MOD_ROWS, d), lambda l, j: (0, 0)),
                  pl.BlockSpec((None, d, tn), lambda l, j: (l, 0, j)),
                  pl.BlockSpec((None, 1, tn), lambda l, j: (l, 0, j))],
        out_specs=pl.BlockSpec((None, MOD_ROWS, tn), lambda l, j: (l, 0, j)),
        out_shape=jax.ShapeDtypeStruct((depth, MOD_ROWS, n), F32),
        compiler_params=_cparams("parallel", "parallel"),
        name="modulation",
    )(cond, w_mod, b_mod.reshape(depth, 1, n))


class _Rows:
    def __init__(self, batch, seq, ctx):
        self.batch, self.seq, self.ctx = batch, seq, ctx
        self.n_lat = batch * seq
        self.n_all = self.n_lat + batch * ctx

    def mod_spec(self, tm, d, slot):
        seq, batch = self.seq, self.batch
        return pl.BlockSpec((None, 1, d),
                            lambda i, *_: (slot * MOD_ROWS + jnp.minimum((i * tm) // seq, batch), 0, 0))


def _ffn_kernel(x_ref, sh_ref, sc_ref, gt_ref, wg_ref, wu_ref, wd_ref, o_ref, h_scr):
    j = pl.program_id(1)

    @pl.when(j == 0)
    def _():
        h = _rms(x_ref[...]) * (1.0 + sc_ref[...]) + sh_ref[...]
        h_scr[...] = h.astype(MXU_DTYPE)

    h = h_scr[...]
    a = _dot(h, wg_ref[...])
    u = _dot(h, wu_ref[...])
    part = _dot((_silu(a) * u).astype(MXU_DTYPE), wd_ref[...])

    @pl.when(j == 0)
    def _():
        o_ref[...] = part

    @pl.when(j > 0)
    def _():
        o_ref[...] += part

    @pl.when(j == pl.num_programs(1) - 1)
    def _():
        o_ref[...] = x_ref[...] + 0.5 * gt_ref[...] * o_ref[...]


def _ffn(x, n_rows, rows, mod, slots, wg, wu, wd, tm):
    d = x.shape[1]
    dff = wg.shape[1]
    tf = 512
    return pl.pallas_call(
        _ffn_kernel,
        grid=(n_rows // tm, dff // tf),
        in_specs=[pl.BlockSpec((tm, d), lambda i, j: (i, 0)),
                  rows.mod_spec(tm, d, slots[0]), rows.mod_spec(tm, d, slots[1]), rows.mod_spec(tm, d, slots[2]),
                  pl.BlockSpec((d, tf), lambda i, j: (0, j)),
                  pl.BlockSpec((d, tf), lambda i, j: (0, j)),
                  pl.BlockSpec((tf, d), lambda i, j: (j, 0))],
        out_specs=pl.BlockSpec((tm, d), lambda i, j: (i, 0)),
        out_shape=jax.ShapeDtypeStruct((n_rows, d), F32),
        scratch_shapes=[pltpu.VMEM((tm, d), MXU_DTYPE)],
        compiler_params=_cparams("parallel", "arbitrary"),
        name="ffn_half_step",
    )(x, mod, mod, mod, wg, wu, wd)


def _proj_kernel(x_ref, sh_ref, sc_ref, w_ref, o_ref, h_scr):
    @pl.when(pl.program_id(1) == 0)
    def _():
        h = _rms(x_ref[...]) * (1.0 + sc_ref[...]) + sh_ref[...]
        h_scr[...] = h.astype(MXU_DTYPE)

    o_ref[...] = _dot(h_scr[...], w_ref[...])


def _proj(x, rows, mod, slots, w, tm, tn):
    d = x.shape[1]
    n = w.shape[1]
    return pl.pallas_call(
        _proj_kernel,
        grid=(rows.n_all // tm, n // tn),
        in_specs=[pl.BlockSpec((tm, d), lambda i, j: (i, 0)),
                  rows.mod_spec(tm, d, slots[0]), rows.mod_spec(tm, d, slots[1]),
                  pl.BlockSpec((d, tn), lambda i, j: (0, j))],
        out_specs=pl.BlockSpec((tm, tn), lambda i, j: (i, j)),
        out_shape=jax.ShapeDtypeStruct((rows.n_all, n), F32),
        scratch_shapes=[pltpu.VMEM((tm, d), MXU_DTYPE)],
        compiler_params=_cparams("parallel", "arbitrary"),
        name="mixer_in_proj",
    )(x, mod, mod, w)


def _out_kernel(n_parts, widths, n_lat_tiles, has_ctx, *refs):
    x_ref, gt_ref, w_ref = refs[0], refs[1], refs[2]
    lat = refs[3:3 + n_parts]
    ctx = refs[3 + n_parts:3 + 2 * n_parts] if has_ctx else ()
    o_ref = refs[-1]

    def run(parts):
        acc = None
        off = 0
        for p, wd in zip(parts, widths):
            t = _dot(p[...], w_ref[off:off + wd, :])
            acc = t if acc is None else acc + t
            off += wd
        o_ref[...] = x_ref[...] + gt_ref[...] * acc

    if has_ctx:
        i = pl.program_id(0)
        pl.when(i < n_lat_tiles)(lambda: run(lat))
        pl.when(i >= n_lat_tiles)(lambda: run(ctx))
    else:
        run(lat)


def _out_proj(x, rows, mod, slot, w, lat_parts, ctx_parts, tm):
    d = x.shape[1]
    has_ctx = ctx_parts is not None
    n_rows = rows.n_all if has_ctx else rows.n_lat
    n_lat_tiles = rows.n_lat // tm
    widths = tuple(p.shape[1] for p in lat_parts)
    specs = [pl.BlockSpec((tm, d), lambda i: (i, 0)), rows.mod_spec(tm, d, slot),
             pl.BlockSpec(w.shape, lambda i: (0, 0))]
    specs += [pl.BlockSpec((tm, wd), lambda i: (jnp.minimum(i, n_lat_tiles - 1), 0)) for wd in widths]
    args = [x, mod, w] + list(lat_parts)
    if has_ctx:
        specs += [pl.BlockSpec((tm, wd), lambda i: (jnp.maximum(i - n_lat_tiles, 0), 0)) for wd in widths]
        args += list(ctx_parts)
    return pl.pallas_call(
        functools.partial(_out_kernel, len(widths), widths, n_lat_tiles, has_ctx),
        grid=(n_rows // tm,),
        in_specs=specs,
        out_specs=pl.BlockSpec((tm, d), lambda i: (i, 0)),
        out_shape=jax.ShapeDtypeStruct((n_rows, d), F32),
        compiler_params=_cparams("parallel"),
        name="mixer_out_proj",
    )(*args)


def _scan_tables(chunk, reverse):
    c = chunk
    levels = [c >> (i + 1) for i in range(int(np.log2(c)))]
    t = np.arange(c)[:, None]
    u = np.arange(c)[None, :]
    whole = np.ones((SCAN_WHOLE_ROWS, c), bool)
    if not reverse:
        groups = [u <= t, u > t, whole]
    else:
        groups = [u >= t, u < t, whole]
    for m in levels:
        base = (t // (2 * m)) * (2 * m)
        hi = (t & m) != 0
        if not reverse:
            r = base + m - 1
            mat = np.where(hi, (u > r) & (u <= t), (u > t) & (u <= r))
        else:
            r = base + m
            mat = np.where(hi, (u >= r) & (u < t), (u >= t) & (u < r))
        groups.append(mat)
    mats = np.concatenate(groups, axis=0).astype(np.float32)
    x = t ^ u
    lvl = np.zeros((c, c), np.int32)
    nz = x > 0
    top = np.zeros_like(x)
    top[nz] = np.floor(np.log2(x[nz])).astype(np.int64)
    lvl[nz] = (int(np.log2(c)) - top[nz]).astype(np.int32)
    valid = (u <= t) if not reverse else (u >= t)
    lvl = np.where(valid, lvl, -1).astype(np.int32)
    return mats, lvl, levels


def _chunk_step(q, k, v, g, st_ref, mats, lvl, reverse):
    c, dk = q.shape
    n_levels = int(np.log2(c))
    g0 = g.astype(MXU_DTYPE)
    r1 = g - g0.astype(F32)
    g1 = r1.astype(MXU_DTYPE)
    g2 = (r1 - g1.astype(F32)).astype(MXU_DTYPE)
    ex3 = _dot(mats, jnp.concatenate([g0, g1, g2], axis=1))
    e = jnp.exp(ex3[:, :dk] + ex3[:, dk:2 * dk] + ex3[:, 2 * dk:])
    e_q, e_k, e_all = e[0:c], e[c:2 * c], e[2 * c:2 * c + 1]
    vb = v.astype(MXU_DTYPE)
    st = st_ref[...]
    o = _dot_nt((q * e_q).astype(MXU_DTYPE), st.astype(MXU_DTYPE))
    attn = jnp.where(lvl == 0, _dot_nt(q.astype(MXU_DTYPE), k.astype(MXU_DTYPE)), 0.0)
    row = lax.broadcasted_iota(jnp.int32, (c, 1), 0)
    lvl0 = 2 * c + SCAN_WHOLE_ROWS
    for li in range(n_levels):
        m = c >> (li + 1)
        e_m = e[lvl0 + li * c:lvl0 + (li + 1) * c]
        is_q = ((row & m) != 0) if not reverse else ((row & m) == 0)
        z = jnp.where(is_q, q, k) * e_m
        zq = jnp.where(is_q, z, 0.0).astype(MXU_DTYPE)
        zk = jnp.where(is_q, 0.0, z).astype(MXU_DTYPE)
        attn = jnp.where(lvl == li + 1, _dot_nt(zq, zk), attn)
    o = o + _dot(attn.astype(MXU_DTYPE), vb)
    st_ref[...] = st * e_all + _dot_tn(vb, (k * e_k).astype(MXU_DTYPE))
    return o


def _bidir_scan(segments, chunk, prep, finish, st_ref, tabs):
    mats_f, lvl_f, mats_b, lvl_b = tabs

    st_ref[...] = jnp.zeros_like(st_ref)
    for n_rows, of_ref, seg in segments:
        def fwd(ci, _, of_ref=of_ref, seg=seg):
            sl = pl.ds(pl.multiple_of(ci * chunk, chunk), chunk)
            q, k, v, g = prep(seg, sl, 0)
            of_ref[sl, :] = _chunk_step(q, k, v, g, st_ref, mats_f, lvl_f, False)
            return 0
        lax.fori_loop(0, n_rows // chunk, fwd, 0)

    st_ref[...] = jnp.zeros_like(st_ref)
    for n_rows, of_ref, seg in segments:
        nck = n_rows // chunk

        def bwd(ci, _, of_ref=of_ref, seg=seg, nck=nck):
            sl = pl.ds(pl.multiple_of((nck - 1 - ci) * chunk, chunk), chunk)
            q, k, v, g = prep(seg, sl, 1)
            o = of_ref[sl, :] + _chunk_step(q, k, v, g, st_ref, mats_b, lvl_b, True)
            finish(seg, sl, o)
            return 0
        lax.fori_loop(0, nck, bwd, 0)


def _log_sigmoid(x):
    return jnp.minimum(x, 0.0) - jnp.log(1.0 + jnp.exp(-jnp.abs(x)))


def _gla_kernel(chunk, seq, ctx, scale,
                mats_f, lvl_f, mats_b, lvl_b, w2_ref, b_ref, gain_ref,
                q_c, k_c, v_c, lr_c, gr_c, q_l, k_l, v_l, lr_l, gr_l,
                y_c, y_l, st_ref, of_c, of_l):
    ins = ((q_c, k_c, v_c, lr_c, gr_c, y_c), (q_l, k_l, v_l, lr_l, gr_l, y_l))

    def prep(seg, sl, direction):
        q_r, k_r, v_r, lr_r = ins[seg][:4]
        logit = _dot(lr_r[sl, :].astype(MXU_DTYPE), w2_ref[direction]) + b_ref[direction]
        g = _log_sigmoid(logit) * (1.0 / GLA_GATE_TAU)
        return q_r[sl, :] * scale, k_r[sl, :], v_r[sl, :], g

    def finish(seg, sl, o):
        gr_r, y_r = ins[seg][4:]
        y_r[sl, :] = (_rms(o) * gain_ref[...] * _silu(gr_r[sl, :])).astype(y_r.dtype)

    _bidir_scan([(ctx, of_c, 0), (seq, of_l, 1)], chunk, prep, finish, st_ref,
                (mats_f[...], lvl_f[...], mats_b[...], lvl_b[...]))


def _hgrn_kernel(chunk, seq, ctx, layer_j,
                 mats_f, lvl_f, mats_b, lvl_b, lbl_ref, gain_ref,
                 q_c, ff_c, fb_c, i_c, hg_c, q_l, ff_l, fb_l, i_l, hg_l,
                 y_c, y_l, st_ref, of_c, of_l):
    ins = ((q_c, (ff_c, fb_c), i_c, hg_c, y_c), (q_l, (ff_l, fb_l), i_l, hg_l, y_l))
    lbs = []
    for direction in range(2):
        logits = lbl_ref[direction]
        pe = jnp.exp(logits - jnp.max(logits, axis=0, keepdims=True))
        p = pe / jnp.sum(pe, axis=0, keepdims=True)
        lbs.append(jnp.sum(p[0:layer_j + 1], axis=0, keepdims=True) - p[0:1])

    def prep(seg, sl, direction):
        q_r, f_rs, i_r = ins[seg][:3]
        lb = lbs[direction]
        x = f_rs[direction][sl, :]
        en = jnp.exp(-jnp.abs(x))
        inv = 1.0 / (1.0 + en)
        pos = x >= 0.0
        sig_p = jnp.where(pos, inv, en * inv)
        sig_n = jnp.where(pos, en * inv, inv)
        f = lb + (1.0 - lb) * sig_p
        g = jnp.log(jnp.maximum(f, F_FLOOR))
        return _silu(q_r[sl, :]), (1.0 - lb) * sig_n, i_r[sl, :], g

    def finish(seg, sl, o):
        hg_r, y_r = ins[seg][3:]
        y_r[sl, :] = (_rms(o) * gain_ref[...] * _silu(hg_r[sl, :])).astype(y_r.dtype)

    _bidir_scan([(ctx, of_c, 0), (seq, of_l, 1)], chunk, prep, finish, st_ref,
                (mats_f[...], lvl_f[...], mats_b[...], lvl_b[...]))


def _scan_table_args(chunk):
    mats_f, lvl_f, _ = _scan_tables(chunk, False)
    mats_b, lvl_b, _ = _scan_tables(chunk, True)
    args = [jnp.asarray(mats_f, MXU_DTYPE), jnp.asarray(lvl_f), jnp.asarray(mats_b, MXU_DTYPE), jnp.asarray(lvl_b)]
    specs = [pl.BlockSpec(a.shape, lambda b, h: (0, 0)) for a in args]
    return args, specs


def _seg_specs(rows, cols):
    ctx_base = rows.n_lat // rows.ctx
    out = []
    for n_rows, base in ((rows.ctx, ctx_base), (rows.seq, 0)):
        for w, c0, per_head in cols:
            out.append(pl.BlockSpec(
                (n_rows, w), lambda b, h, base=base, cb=c0 // w, ph=int(per_head): (base + b, cb + ph * h)))
    return out


def _gla_mixer(p, rows, cols, w2p, b2, gain, dk, dv):
    chunk = SCAN_CHUNK
    targs, tspecs = _scan_table_args(chunk)
    u = LANES
    seg = _seg_specs(rows, [(dk, cols["gq"], True), (dk, cols["gk"], True), (dv, cols["gv"], True),
                            (u, cols["glr"], False), (dv, cols["gr"], True)])
    specs = tspecs + [pl.BlockSpec((2, u, dk), lambda b, h: (0, 0, h)),
                      pl.BlockSpec((2, 1, dk), lambda b, h: (0, 0, h)),
                      pl.BlockSpec((1, dv), lambda b, h: (0, 0))] + seg
    width = GLA_HEADS * dv
    y_c, y_l = pl.pallas_call(
        functools.partial(_gla_kernel, chunk, rows.seq, rows.ctx, dk ** -0.5),
        grid=(rows.batch, GLA_HEADS),
        in_specs=specs,
        out_specs=[pl.BlockSpec((rows.ctx, dv), lambda b, h: (b, h)),
                   pl.BlockSpec((rows.seq, dv), lambda b, h: (b, h))],
        out_shape=[jax.ShapeDtypeStruct((rows.batch * rows.ctx, width), MXU_DTYPE),
                   jax.ShapeDtypeStruct((rows.n_lat, width), MXU_DTYPE)],
        scratch_shapes=[pltpu.VMEM((dv, dk), F32), pltpu.VMEM((rows.ctx, dv), F32), pltpu.VMEM((rows.seq, dv), F32)],
        compiler_params=_cparams("parallel", "parallel"),
        name="gla_scan",
    )(*targs, w2p, b2, gain, *([p] * 10))
    return y_l, y_c


def _hgrn_mixer(p, rows, cols, lb_logits, gain, layer_j, n_heads, dk):
    chunk = SCAN_CHUNK
    targs, tspecs = _scan_table_args(chunk)
    col_list = [(dk, cols["hq"], True), (dk, cols["hf"], True), (dk, cols["hf"] + n_heads * dk, True),
                (dk, cols["hi"], True), (dk, cols["hg"], True)]
    n_even = lb_logits.shape[1]
    specs = tspecs + [pl.BlockSpec((2, n_even, dk), lambda b, h: (0, 0, h)),
                      pl.BlockSpec((1, dk), lambda b, h: (0, 0))] + _seg_specs(rows, col_list)
    width = n_heads * dk
    y_c, y_l = pl.pallas_call(
        functools.partial(_hgrn_kernel, chunk, rows.seq, rows.ctx, layer_j),
        grid=(rows.batch, n_heads),
        in_specs=specs,
        out_specs=[pl.BlockSpec((rows.ctx, dk), lambda b, h: (b, h)),
                   pl.BlockSpec((rows.seq, dk), lambda b, h: (b, h))],
        out_shape=[jax.ShapeDtypeStruct((rows.batch * rows.ctx, width), MXU_DTYPE),
                   jax.ShapeDtypeStruct((rows.n_lat, width), MXU_DTYPE)],
        scratch_shapes=[pltpu.VMEM((dk, dk), F32), pltpu.VMEM((rows.ctx, dk), F32), pltpu.VMEM((rows.seq, dk), F32)],
        compiler_params=_cparams("parallel", "parallel"),
        name="hgrn_scan",
    )(*targs, lb_logits, gain, *([p] * 10))
    return y_l, y_c


def _qkv_kernel(n_q, n_kv, p_ref, cos_ref, sin_ref, qg_ref, kg_ref, q_ref, k_ref, v_ref):
    hd = ATTN_HEAD_DIM
    cos, sin = cos_ref[...], sin_ref[...]
    lane = lax.broadcasted_iota(jnp.int32, (1, hd), 1)
    first = (lane % (hd // 2)) < (hd // 4)

    def rope(a):
        rot = jnp.where(first, pltpu.roll(a, hd - hd // 4, axis=1), pltpu.roll(a, hd // 4, axis=1))
        return a * cos + rot * sin

    for h in range(n_q):
        a = _rms(p_ref[:, h * hd:(h + 1) * hd]) * qg_ref[...] * (hd ** -0.5)
        q_ref[:, h * hd:(h + 1) * hd] = rope(a).astype(q_ref.dtype)
    for h in range(n_kv):
        a = _rms(p_ref[:, (n_q + h) * hd:(n_q + h + 1) * hd]) * kg_ref[...]
        k_ref[:, h * hd:(h + 1) * hd] = rope(a).astype(k_ref.dtype)
    v_ref[...] = p_ref[:, (n_q + n_kv) * hd:].astype(v_ref.dtype)


def _qkv_prep(p, rows, cos, sin, qg, kg, n_q, n_kv, tm):
    hd = ATTN_HEAD_DIM
    n_lat_tiles = rows.n_lat // tm
    per_seq = rows.seq // tm
    tab = pl.BlockSpec((tm, hd), lambda i: (jnp.where(i < n_lat_tiles, i % per_seq, per_seq), 0))
    return pl.pallas_call(
        functools.partial(_qkv_kernel, n_q, n_kv),
        grid=(rows.n_all // tm,),
        in_specs=[pl.BlockSpec((tm, p.shape[1]), lambda i: (i, 0)), tab, tab,
                  pl.BlockSpec((1, hd), lambda i: (0, 0)), pl.BlockSpec((1, hd), lambda i: (0, 0))],
        out_specs=[pl.BlockSpec((tm, n_q * hd), lambda i: (i, 0)),
                   pl.BlockSpec((tm, n_kv * hd), lambda i: (i, 0)),
                   pl.BlockSpec((tm, n_kv * hd), lambda i: (i, 0))],
        out_shape=[jax.ShapeDtypeStruct((rows.n_all, n_q * hd), MXU_DTYPE),
                   jax.ShapeDtypeStruct((rows.n_all, n_kv * hd), MXU_DTYPE),
                   jax.ShapeDtypeStruct((rows.n_all, n_kv * hd), MXU_DTYPE)],
        compiler_params=_cparams("parallel"),
        name="qkv_norm_rope",
    )(p, cos, sin, qg, kg)


def _attn_kernel(n_kv_sets, q_ref, *refs):
    kv = refs[:2 * n_kv_sets]
    o_ref = refs[2 * n_kv_sets]
    hd = ATTN_HEAD_DIM
    for g in range(ATTN_GROUP):
        q = q_ref[:, g * hd:(g + 1) * hd]
        s = [_dot_nt(q, kv[2 * i][...]) for i in range(n_kv_sets)]
        m = functools.reduce(jnp.maximum, [jnp.max(si, axis=-1, keepdims=True) for si in s])
        p = [jnp.exp(si - m) for si in s]
        den = functools.reduce(jnp.add, [jnp.sum(pi, axis=-1, keepdims=True) for pi in p])
        acc = functools.reduce(jnp.add, [_dot(p[i].astype(MXU_DTYPE), kv[2 * i + 1][...]) for i in range(n_kv_sets)])
        o_ref[:, g * hd:(g + 1) * hd] = (acc / den).astype(o_ref.dtype)


def _attention(q, k, v, rows, n_kv, tq, ctx_queries):
    hd = ATTN_HEAD_DIM
    gw = ATTN_GROUP * hd
    ctx_base = rows.n_lat // rows.ctx
    ctx_kv = pl.BlockSpec((rows.ctx, hd), lambda b, h, i: (ctx_base + b, h))
    if ctx_queries:
        n_q, per = rows.batch * rows.ctx, 1
        tq = rows.ctx
        q_spec = pl.BlockSpec((tq, gw), lambda b, h, i: (ctx_base + b, h))
        kv_specs, kv_args = [ctx_kv, ctx_kv], [k, v]
    else:
        n_q, per = rows.n_lat, rows.seq // tq
        q_spec = pl.BlockSpec((tq, gw), lambda b, h, i: (b * per + i, h))
        lat_kv = pl.BlockSpec((rows.seq, hd), lambda b, h, i: (b, h))
        kv_specs, kv_args = [lat_kv, lat_kv, ctx_kv, ctx_kv], [k, v, k, v]
    return pl.pallas_call(
        functools.partial(_attn_kernel, len(kv_args) // 2),
        grid=(rows.batch, n_kv, per),
        in_specs=[q_spec] + kv_specs,
        out_specs=pl.BlockSpec((tq, gw), lambda b, h, i: (b * per + i, h)),
        out_shape=jax.ShapeDtypeStruct((n_q, n_kv * gw), MXU_DTYPE),
        compiler_params=_cparams("parallel", "parallel", "arbitrary"),
        name="gqa_ctx" if ctx_queries else "gqa_latent",
    )(q, *kv_args)


def _rope_tables(seq, pad_rows):
    hd = ATTN_HEAD_DIM
    half = hd // 2
    pos = jnp.arange(seq)
    inv_freq = ROPE_THETA ** (-jnp.arange(0, half, 2, dtype=F32) / half)
    ang = jnp.stack([pos // GRID_W, pos % GRID_W], axis=-1).astype(F32)[:, :, None] * inv_freq
    cos = jnp.concatenate([jnp.cos(ang), jnp.cos(ang)], axis=-1).reshape(seq, hd)
    sin = jnp.concatenate([-jnp.sin(ang), jnp.sin(ang)], axis=-1).reshape(seq, hd)
    cos = jnp.concatenate([cos, jnp.ones((pad_rows, hd), F32)], axis=0)
    sin = jnp.concatenate([sin, jnp.zeros((pad_rows, hd), F32)], axis=0)
    return cos, sin


def _final_kernel(x_ref, g_ref, o_ref):
    o_ref[...] = _rms(x_ref[...]) * g_ref[...]


def _final_norm(x, gain, tm):
    n, d = x.shape
    return pl.pallas_call(
        _final_kernel,
        grid=(n // tm,),
        in_specs=[pl.BlockSpec((tm, d), lambda i: (i, 0)), pl.BlockSpec((1, d), lambda i: (0, 0))],
        out_specs=pl.BlockSpec((tm, d), lambda i: (i, 0)),
        out_shape=jax.ShapeDtypeStruct((n, d), F32),
        compiler_params=_cparams("parallel"),
        name="final_norm",
    )(x, gain.reshape(1, d))


def _even_layout(d):
    gla_w = d // 2
    gla_qk = gla_w // 2
    hgrn_w = d // 2
    names = ["gq", "gk", "gv", "gr", "glr", "hq", "hf", "hi", "hg"]
    widths = [gla_qk, gla_qk, gla_w, gla_w, MXU_N, hgrn_w, 2 * hgrn_w, hgrn_w, hgrn_w]
    offs = np.concatenate([[0], np.cumsum(widths)])
    return dict(zip(names, (int(o) for o in offs[:-1]))), int(offs[-1])


def kernel(x, c, ctx, c_ctx, w_mod, b_mod, ffn_w_gate, ffn_w_up, ffn_w_down, mix_ab_w_in, gla_gate_w2, gla_gate_b,
           gla_norm_g, hgrn_lb_logits, hgrn_norm_g, mix_ab_w_out, attn_w_in, attn_q_norm_g, attn_k_norm_g,
           attn_w_out, final_norm_g):
    batch, seq, d = x.shape
    n_ctx = ctx.shape[1]
    depth = w_mod.shape[0]
    rows = _Rows(batch, seq, n_ctx)
    tm = _row_tile(seq, batch * n_ctx, 512)
    bf = MXU_DTYPE

    gla_w = d // 2
    gla_dv = gla_w // GLA_HEADS
    gla_dk = gla_dv // 2
    hgrn_w = d // 2
    hgrn_heads = hgrn_w // HGRN_EXPAND
    n_q_heads = d // ATTN_HEAD_DIM
    n_kv_heads = n_q_heads // ATTN_GROUP

    cond = jnp.zeros((MOD_ROWS, d), F32).at[:batch].set(c).at[batch].set(c_ctx)
    mod = _modulation(cond, w_mod, b_mod)
    mod = mod.reshape(depth, MOD_ROWS, N_MOD, d).transpose(0, 2, 1, 3).reshape(depth, N_MOD * MOD_ROWS, 1, d)

    cols, n_even_cols = _even_layout(d)
    cos, sin = _rope_tables(seq, tm)

    xs = jnp.concatenate([x.reshape(batch * seq, d), ctx.reshape(batch * n_ctx, d)], axis=0)
    for layer in range(depth):
        last = layer == depth - 1
        j = layer // 2
        m = mod[layer]
        xs = _ffn(xs, rows.n_all, rows, m, (0, 1, 2), ffn_w_gate[layer, 0].astype(bf), ffn_w_up[layer, 0].astype(bf),
                  ffn_w_down[layer, 0].astype(bf), tm)
        if layer % 2 == 0:
            w_in = mix_ab_w_in[j]
            lr0 = cols["glr"]
            lr_w = 2 * GLA_GATE_RANK
            w_pad = jnp.concatenate([w_in[:, :lr0 + lr_w], jnp.zeros((d, MXU_N - lr_w), F32), w_in[:, lr0 + lr_w:]],
                                    axis=1).astype(bf)
            p = _proj(xs, rows, m, (3, 4), w_pad, tm, 768 if n_even_cols % 768 == 0 else MXU_N)
            w2p = jnp.zeros((2, LANES, gla_w // 2), F32)
            for direction in range(2):
                r0 = direction * GLA_GATE_RANK
                w2p = w2p.at[direction, r0:r0 + GLA_GATE_RANK].set(gla_gate_w2[j, direction])
            ya_l, ya_c = _gla_mixer(p, rows, cols, w2p.astype(bf), gla_gate_b[j].reshape(2, 1, -1),
                                    gla_norm_g[j].reshape(1, -1), gla_dk, gla_dv)
            yb_l, yb_c = _hgrn_mixer(p, rows, cols, hgrn_lb_logits, hgrn_norm_g[j].reshape(1, -1), j,
                                     hgrn_heads, HGRN_EXPAND)
            lat_parts, ctx_parts = [ya_l, yb_l], [ya_c, yb_c]
            w_out = mix_ab_w_out[j].astype(bf)
        else:
            p = _proj(xs, rows, m, (3, 4), attn_w_in[j].astype(bf), tm, 768)
            q, k, v = _qkv_prep(p, rows, cos, sin, attn_q_norm_g[j].reshape(1, -1), attn_k_norm_g[j].reshape(1, -1),
                                n_q_heads, n_kv_heads, tm)
            lat_parts = [_attention(q, k, v, rows, n_kv_heads, min(256, seq), False)]
            ctx_parts = None if last else [_attention(q, k, v, rows, n_kv_heads, n_ctx, True)]
            w_out = attn_w_out[j].astype(bf)
        if last:
            ctx_parts = None
        xs = _out_proj(xs, rows, m, 5, w_out, lat_parts, ctx_parts, tm)
        n_rows = rows.n_lat if last else rows.n_all
        xs = _ffn(xs, n_rows, rows, m, (6, 7, 8), ffn_w_gate[layer, 1].astype(bf), ffn_w_up[layer, 1].astype(bf),
                  ffn_w_down[layer, 1].astype(bf), tm)
    out = _final_norm(xs[:rows.n_lat], final_norm_g, tm)
    return out.reshape(batch, seq, d)
```

```python
import functools

import jax
import jax.numpy as jnp
import numpy as np
from jax import lax
from jax.experimental import pallas as pl
from jax.experimental.pallas import tpu as pltpu

F32 = jnp.float32
MXU_DTYPE = jnp.bfloat16

EPS = 1e-6
F_FLOOR = 1e-30
N_MOD = 9
GRID_W = 64
GLA_HEADS = 4
GLA_GATE_RANK = 16
GLA_GATE_TAU = 16.0
HGRN_EXPAND = 128
ATTN_HEAD_DIM = 128
ATTN_GROUP = 4
ROPE_THETA = 10000.0

LANES = 128
MXU_N = 256
VMEM_LIMIT_BYTES = 56 * 1024 * 1024

SCAN_CHUNK = 256
SUBLANES = 8
LOG2E = 1.4426950408889634
MOD_ROWS = 16


def _cparams(*sem):
    return pltpu.CompilerParams(dimension_semantics=sem, vmem_limit_bytes=VMEM_LIMIT_BYTES)


def _sigmoid(x):
    return 1.0 / (1.0 + jnp.exp(-x))


def _silu(x):
    return x * _sigmoid(x)


def _rms(x):
    return x * lax.rsqrt(jnp.mean(x * x, axis=-1, keepdims=True) + EPS)


def _dot(a, b):
    return jnp.dot(a, b, preferred_element_type=F32)


def _dot_nt(a, b):
    return lax.dot_general(a, b, (((1,), (1,)), ((), ())), preferred_element_type=F32)


def _dot_tn(a, b):
    return lax.dot_general(a, b, (((0,), (0,)), ((), ())), preferred_element_type=F32)


def _row_tile(n_lat, n_ctx_total, cap):
    t = cap
    while n_lat % t or n_ctx_total % t:
        t //= 2
    return t


def _col_tile(n, cap=1024):
    return max(t for t in range(MXU_N, cap + 1, MXU_N) if n % t == 0)


def _mod_kernel(c_ref, w_ref, b_ref, o_ref):
    s = _silu(c_ref[...]).astype(MXU_DTYPE)
    o_ref[...] = _dot(s, w_ref[...].astype(MXU_DTYPE)) + b_ref[...]


def _modulation(cond, w_mod, b_mod):
    depth, d, n = w_mod.shape
    tn = 1024
    return pl.pallas_call(
        _mod_kernel,
        grid=(depth, n // tn),
        in_specs=[pl.BlockSpec((MOD_ROWS, d), lambda l, j: (0, 0)),
                  pl.BlockSpec((None, d, tn), lambda l, j: (l, 0, j)),
                  pl.BlockSpec((None, 1, tn), lambda l, j: (l, 0, j))],
        out_specs=pl.BlockSpec((None, MOD_ROWS, tn), lambda l, j: (l, 0, j)),
        out_shape=jax.ShapeDtypeStruct((depth, MOD_ROWS, n), F32),
        compiler_params=_cparams("parallel", "parallel"),
        name="modulation",
    )(cond, w_mod, b_mod.reshape(depth, 1, n))


class _Rows:
    def __init__(self, batch, seq, ctx):
        self.batch, self.seq, self.ctx = batch, seq, ctx
        self.n_lat = batch * seq
        self.n_all = self.n_lat + batch * ctx

    def mod_spec(self, tm, width, slot, by_column=False):
        seq, batch = self.seq, self.batch

        def row(i):
            return slot * MOD_ROWS + jnp.minimum((i * tm) // seq, batch)
        if by_column:
            return pl.BlockSpec((None, 1, width), lambda i, j: (row(i), 0, j))
        return pl.BlockSpec((None, 1, width), lambda i, *_: (row(i), 0, 0))


def _ffn_up_kernel(x_ref, sh_ref, sc_ref, wg_ref, wu_ref, o_ref, h_scr):
    @pl.when(pl.program_id(1) == 0)
    def _():
        h = _rms(x_ref[...]) * (1.0 + sc_ref[...]) + sh_ref[...]
        h_scr[...] = h.astype(MXU_DTYPE)

    h = h_scr[...]
    o_ref[...] = (_silu(_dot(h, wg_ref[...])) * _dot(h, wu_ref[...])).astype(o_ref.dtype)


def _ffn_down_kernel(a_ref, wd_ref, x_ref, gt_ref, o_ref):
    o_ref[...] = x_ref[...] + 0.5 * gt_ref[...] * _dot(a_ref[...], wd_ref[...])


def _ffn(x, n_rows, rows, mod, slots, wg, wu, wd, tm):
    d = x.shape[1]
    dff = wg.shape[1]
    tf = _col_tile(dff, 512)
    act = pl.pallas_call(
        _ffn_up_kernel,
        grid=(n_rows // tm, dff // tf),
        in_specs=[pl.BlockSpec((tm, d), lambda i, j: (i, 0)),
                  rows.mod_spec(tm, d, slots[0]), rows.mod_spec(tm, d, slots[1]),
                  pl.BlockSpec((d, tf), lambda i, j: (0, j)),
                  pl.BlockSpec((d, tf), lambda i, j: (0, j))],
        out_specs=pl.BlockSpec((tm, tf), lambda i, j: (i, j)),
        out_shape=jax.ShapeDtypeStruct((n_rows, dff), MXU_DTYPE),
        scratch_shapes=[pltpu.VMEM((tm, d), MXU_DTYPE)],
        compiler_params=_cparams("parallel", "arbitrary"),
        name="ffn_up",
    )(x, mod, mod, wg, wu)
    tn = _col_tile(d, 512)
    return pl.pallas_call(
        _ffn_down_kernel,
        grid=(n_rows // tm, d // tn),
        in_specs=[pl.BlockSpec((tm, dff), lambda i, j: (i, 0)),
                  pl.BlockSpec((dff, tn), lambda i, j: (0, j)),
                  pl.BlockSpec((tm, tn), lambda i, j: (i, j)),
                  rows.mod_spec(tm, tn, slots[2], by_column=True)],
        out_specs=pl.BlockSpec((tm, tn), lambda i, j: (i, j)),
        out_shape=jax.ShapeDtypeStruct((n_rows, d), F32),
        compiler_params=_cparams("parallel", "arbitrary"),
        name="ffn_down",
    )(act, wd, x, mod)


def _proj_kernel(x_ref, sh_ref, sc_ref, w_ref, o_ref, h_scr):
    @pl.when(pl.program_id(1) == 0)
    def _():
        h = _rms(x_ref[...]) * (1.0 + sc_ref[...]) + sh_ref[...]
        h_scr[...] = h.astype(MXU_DTYPE)

    o_ref[...] = _dot(h_scr[...], w_ref[...]).astype(o_ref.dtype)


def _proj(x, rows, mod, slots, w, tm, tn, out_dtype=F32):
    d = x.shape[1]
    n = w.shape[1]
    return pl.pallas_call(
        _proj_kernel,
        grid=(rows.n_all // tm, n // tn),
        in_specs=[pl.BlockSpec((tm, d), lambda i, j: (i, 0)),
                  rows.mod_spec(tm, d, slots[0]), rows.mod_spec(tm, d, slots[1]),
                  pl.BlockSpec((d, tn), lambda i, j: (0, j))],
        out_specs=pl.BlockSpec((tm, tn), lambda i, j: (i, j)),
        out_shape=jax.ShapeDtypeStruct((rows.n_all, n), out_dtype),
        scratch_shapes=[pltpu.VMEM((tm, d), MXU_DTYPE)],
        compiler_params=_cparams("parallel", "arbitrary"),
        name="mixer_in_proj",
    )(x, mod, mod, w)


def _out_kernel(n_parts, widths, n_lat_tiles, has_ctx, *refs):
    x_ref, gt_ref, w_ref = refs[0], refs[1], refs[2]
    lat = refs[3:3 + n_parts]
    ctx = refs[3 + n_parts:3 + 2 * n_parts] if has_ctx else ()
    o_ref = refs[-1]

    def run(parts):
        acc = None
        off = 0
        for p, wd in zip(parts, widths):
            t = _dot(p[...], w_ref[off:off + wd, :])
            acc = t if acc is None else acc + t
            off += wd
        o_ref[...] = x_ref[...] + gt_ref[...] * acc

    if has_ctx:
        i = pl.program_id(0)
        pl.when(i < n_lat_tiles)(lambda: run(lat))
        pl.when(i >= n_lat_tiles)(lambda: run(ctx))
    else:
        run(lat)


def _out_proj(x, rows, mod, slot, w, lat_parts, ctx_parts, tm):
    d = x.shape[1]
    has_ctx = ctx_parts is not None
    n_rows = rows.n_all if has_ctx else rows.n_lat
    n_lat_tiles = rows.n_lat // tm
    widths = tuple(p.shape[1] for p in lat_parts)
    specs = [pl.BlockSpec((tm, d), lambda i: (i, 0)), rows.mod_spec(tm, d, slot),
             pl.BlockSpec(w.shape, lambda i: (0, 0))]
    specs += [pl.BlockSpec((tm, wd), lambda i: (jnp.minimum(i, n_lat_tiles - 1), 0)) for wd in widths]
    args = [x, mod, w] + list(lat_parts)
    if has_ctx:
        specs += [pl.BlockSpec((tm, wd), lambda i: (jnp.maximum(i - n_lat_tiles, 0), 0)) for wd in widths]
        args += list(ctx_parts)
    return pl.pallas_call(
        functools.partial(_out_kernel, len(widths), widths, n_lat_tiles, has_ctx),
        grid=(n_rows // tm,),
        in_specs=specs,
        out_specs=pl.BlockSpec((tm, d), lambda i: (i, 0)),
        out_shape=jax.ShapeDtypeStruct((n_rows, d), F32),
        compiler_params=_cparams("parallel"),
        name="mixer_out_proj",
    )(*args)


def _scan_tables(chunk):
    c = chunk
    t = np.arange(c)[:, None]
    u = np.arange(c)[None, :]
    tri = (u <= t).astype(np.float32)
    x = t ^ u
    lvl = np.zeros((c, c), np.int32)
    nz = x > 0
    lvl[nz] = int(np.log2(c)) - np.floor(np.log2(x[nz])).astype(np.int32)
    return tri, np.where(u <= t, lvl, -1).astype(np.int32), np.where(u >= t, lvl, -1).astype(np.int32)


def _level_ref(cum, m):
    c, dk = cum.shape
    if 2 * m >= SUBLANES:
        x = cum.reshape(c // (2 * m), 2 * m, dk)
        return jnp.broadcast_to(x[:, m - 1:m, :], x.shape).reshape(c, dk)
    x = cum.reshape(c // SUBLANES, SUBLANES, dk)
    sub = lax.broadcasted_iota(jnp.int32, x.shape, 1)
    lo, hi = (jnp.broadcast_to(x[:, r:r + 1, :], x.shape) for r in (m - 1, 2 * m + m - 1))
    return jnp.where(sub < 2 * m, lo, hi).reshape(c, dk)


def _chunk_local(q, k, vb, g, tri, lvl, reverse):
    c, dk = q.shape
    n_levels = int(np.log2(c))
    g = g * LOG2E
    g0 = g.astype(MXU_DTYPE)
    r1 = g - g0.astype(F32)
    g1 = r1.astype(MXU_DTYPE)
    g2 = (r1 - g1.astype(F32)).astype(MXU_DTYPE)
    ex3 = _dot(tri, jnp.concatenate([g0, g1, g2], axis=1))
    cum = ex3[:, :dk] + ex3[:, dk:2 * dk] + ex3[:, 2 * dk:]
    total = cum[c - 1:c]
    if not reverse:
        a, e_q, e_k = cum, cum, total - cum
    else:
        a = cum - g
        e_q, e_k = total - a, a
    attn = jnp.where(lvl == 0, _dot_nt(q.astype(MXU_DTYPE), k.astype(MXU_DTYPE)), 0.0)
    row = lax.broadcasted_iota(jnp.int32, (c, 1), 0)
    for li in range(n_levels):
        m = c >> (li + 1)
        second = (row & m) != 0
        is_q = jnp.logical_not(second) if reverse else second
        if m == 1:
            e_m = jnp.where(is_q, g, 0.0)
        else:
            d = a - _level_ref(cum, m)
            e_m = jnp.where(second, d, -d)
        z = jnp.where(is_q, q, k) * jnp.exp2(e_m)
        zq = jnp.where(is_q, z, 0.0).astype(MXU_DTYPE)
        zk = jnp.where(is_q, 0.0, z).astype(MXU_DTYPE)
        attn = jnp.where(lvl == li + 1, _dot_nt(zq, zk), attn)
    o = _dot(attn.astype(MXU_DTYPE), vb)
    return o, (q * jnp.exp2(e_q)).astype(MXU_DTYPE), (k * jnp.exp2(e_k)).astype(MXU_DTYPE), jnp.exp2(total)


def _bidir_scan(segments, chunk, gates, finish, tabs, scr):
    tri, lvl_f, lvl_b = tabs
    acc, vt_s, qh_s, kh_s, dec_s, st_s = scr
    n_seg = [n_rows // chunk for n_rows, _ in segments]
    n_all = sum(n_seg)

    def rows_of(j):
        return pl.ds(pl.multiple_of(j * chunk, chunk), chunk)

    base = 0
    for (n_rows, seg), n in zip(segments, n_seg):
        def local(ci, _, seg=seg, base=base):
            q, ks, v, gs = gates(seg, rows_of(ci))
            j = base + ci
            vb = v.astype(MXU_DTYPE)
            vt_s[j] = vb.T
            o = None
            for d, lvl in enumerate((lvl_f, lvl_b)):
                o_d, qh, kh, dec = _chunk_local(q, ks[d], vb, gs[d], tri, lvl, d == 1)
                qh_s[d, rows_of(j), :] = qh
                kh_s[d, rows_of(j), :] = kh
                dec_s[d, pl.ds(pl.multiple_of(j * SUBLANES, SUBLANES), SUBLANES), :] = jnp.broadcast_to(
                    dec, (SUBLANES, dec.shape[1]))
                o = o_d if o is None else o + o_d
            acc[rows_of(j), :] = o
            return 0
        lax.fori_loop(0, n, local, 0)
        base += n

    st_s[...] = jnp.zeros_like(st_s)
    n_first = n_seg[0]

    def carry(i, _):
        j_b = jnp.where(i < n_first, n_first - 1 - i, n_all + n_first - 1 - i)
        for d, j in enumerate((i, j_b)):
            st = st_s[d]
            acc[rows_of(j), :] += _dot_nt(qh_s[d, rows_of(j), :], st.astype(MXU_DTYPE))
            dec = dec_s[d, pl.ds(pl.multiple_of(j * SUBLANES, SUBLANES), 1), :]
            st_s[d] = st * dec + _dot(vt_s[j], kh_s[d, rows_of(j), :])
        return 0
    lax.fori_loop(0, n_all, carry, 0)

    base = 0
    for (n_rows, seg), n in zip(segments, n_seg):
        def done(ci, _, seg=seg, base=base):
            finish(seg, rows_of(ci), acc[rows_of(base + ci), :])
            return 0
        lax.fori_loop(0, n, done, 0)
        base += n


def _log_sigmoid(x):
    return jnp.minimum(x, 0.0) - jnp.log(1.0 + jnp.exp(-jnp.abs(x)))


def _gla_kernel(chunk, seq, ctx, scale,
                tri, lvl_f, lvl_b, w2_ref, b_ref, gain_ref,
                q_c, k_c, v_c, gr_c, lr_c, q_l, k_l, v_l, gr_l, lr_l,
                y_c, y_l, *scr):
    ins = ((q_c, k_c, v_c, lr_c, gr_c, y_c), (q_l, k_l, v_l, lr_l, gr_l, y_l))

    def gates(seg, sl):
        q_r, k_r, v_r, lr_r = ins[seg][:4]
        lr = lr_r[sl, :].astype(MXU_DTYPE)
        gs = tuple(_log_sigmoid(_dot(lr, w2_ref[d]) + b_ref[d]) * (1.0 / GLA_GATE_TAU) for d in range(2))
        k = k_r[sl, :].astype(F32)
        return q_r[sl, :].astype(F32) * scale, (k, k), v_r[sl, :], gs

    def finish(seg, sl, o):
        gr_r, y_r = ins[seg][4:]
        y_r[sl, :] = (_rms(o) * gain_ref[...] * _silu(gr_r[sl, :].astype(F32))).astype(y_r.dtype)

    _bidir_scan([(ctx, 0), (seq, 1)], chunk, gates, finish, (tri[...], lvl_f[...], lvl_b[...]), scr)


def _hgrn_kernel(chunk, seq, ctx, layer_j,
                 tri, lvl_f, lvl_b, lbl_ref, gain_ref,
                 q_c, i_c, hg_c, ff_c, fb_c, q_l, i_l, hg_l, ff_l, fb_l,
                 y_c, y_l, *scr):
    ins = ((q_c, (ff_c, fb_c), i_c, hg_c, y_c), (q_l, (ff_l, fb_l), i_l, hg_l, y_l))
    lbs = []
    for direction in range(2):
        logits = lbl_ref[direction]
        pe = jnp.exp(logits - jnp.max(logits, axis=0, keepdims=True))
        p = pe / jnp.sum(pe, axis=0, keepdims=True)
        lbs.append(jnp.sum(p[0:layer_j + 1], axis=0, keepdims=True) - p[0:1])

    def gates(seg, sl):
        q_r, f_rs, i_r = ins[seg][:3]
        ks, gs = [], []
        for d in range(2):
            lb = lbs[d]
            x = f_rs[d][sl, :]
            en = jnp.exp(-jnp.abs(x))
            inv = 1.0 / (1.0 + en)
            pos = x >= 0.0
            sig_p = jnp.where(pos, inv, en * inv)
            sig_n = jnp.where(pos, en * inv, inv)
            gs.append(jnp.log(jnp.maximum(lb + (1.0 - lb) * sig_p, F_FLOOR)))
            ks.append((1.0 - lb) * sig_n)
        return _silu(q_r[sl, :].astype(F32)), ks, i_r[sl, :], gs

    def finish(seg, sl, o):
        hg_r, y_r = ins[seg][3:]
        y_r[sl, :] = (_rms(o) * gain_ref[...] * _silu(hg_r[sl, :].astype(F32))).astype(y_r.dtype)

    _bidir_scan([(ctx, 0), (seq, 1)], chunk, gates, finish, (tri[...], lvl_f[...], lvl_b[...]), scr)


def _scan_scratch(chunk, n_rows, dk, dv):
    n_chunks = n_rows // chunk
    return [pltpu.VMEM((n_rows, dv), F32), pltpu.VMEM((n_chunks, dv, chunk), MXU_DTYPE),
            pltpu.VMEM((2, n_rows, dk), MXU_DTYPE), pltpu.VMEM((2, n_rows, dk), MXU_DTYPE),
            pltpu.VMEM((2, n_chunks * SUBLANES, dk), F32), pltpu.VMEM((2, dv, dk), F32)]


def _scan_chunk(rows):
    return _row_tile(rows.seq, rows.ctx, SCAN_CHUNK)


def _scan_table_args(chunk):
    tri, lvl_f, lvl_b = _scan_tables(chunk)
    args = [jnp.asarray(tri, MXU_DTYPE), jnp.asarray(lvl_f), jnp.asarray(lvl_b)]
    specs = [pl.BlockSpec(a.shape, lambda b, h: (0, 0)) for a in args]
    return args, specs


def _seg_inputs(rows, cols):
    ctx_base = rows.n_lat // rows.ctx
    specs, arrays = [], []
    for n_rows, base in ((rows.ctx, ctx_base), (rows.seq, 0)):
        for arr, w, c0, per_head in cols:
            specs.append(pl.BlockSpec(
                (n_rows, w), lambda b, h, base=base, cb=c0 // w, ph=int(per_head): (base + b, cb + ph * h)))
            arrays.append(arr)
    return specs, arrays


def _scan_call(body, name, rows, n_heads, dk, dv, chunk, param_specs, params, seg_cols):
    targs, tspecs = _scan_table_args(chunk)
    seg_specs, seg_arrays = _seg_inputs(rows, seg_cols)
    width = n_heads * dv
    y_c, y_l = pl.pallas_call(
        body,
        grid=(rows.batch, n_heads),
        in_specs=tspecs + param_specs + seg_specs,
        out_specs=[pl.BlockSpec((rows.ctx, dv), lambda b, h: (b, h)),
                   pl.BlockSpec((rows.seq, dv), lambda b, h: (b, h))],
        out_shape=[jax.ShapeDtypeStruct((rows.batch * rows.ctx, width), MXU_DTYPE),
                   jax.ShapeDtypeStruct((rows.n_lat, width), MXU_DTYPE)],
        scratch_shapes=_scan_scratch(chunk, rows.ctx + rows.seq, dk, dv),
        compiler_params=_cparams("parallel", "parallel"),
        name=name,
    )(*targs, *params, *seg_arrays)
    return y_l, y_c


def _gla_mixer(p16, p32, rows, c16, c32, w2p, b2, gain, dk, dv):
    chunk = _scan_chunk(rows)
    u = LANES
    param_specs = [pl.BlockSpec((2, u, dk), lambda b, h: (0, 0, h)),
                   pl.BlockSpec((2, 1, dk), lambda b, h: (0, 0, h)),
                   pl.BlockSpec((1, dv), lambda b, h: (0, 0))]
    seg_cols = [(p16, dk, c16["gq"], True), (p16, dk, c16["gk"], True), (p16, dv, c16["gv"], True),
                (p16, dv, c16["gr"], True), (p32, u, c32["glr"], False)]
    return _scan_call(functools.partial(_gla_kernel, chunk, rows.seq, rows.ctx, dk ** -0.5), "gla_scan",
                      rows, GLA_HEADS, dk, dv, chunk, param_specs, [w2p, b2, gain], seg_cols)


def _hgrn_mixer(p16, p32, rows, c16, c32, lb_logits, gain, layer_j, n_heads, dk):
    chunk = _scan_chunk(rows)
    n_even = lb_logits.shape[1]
    param_specs = [pl.BlockSpec((2, n_even, dk), lambda b, h: (0, 0, h)),
                   pl.BlockSpec((1, dk), lambda b, h: (0, 0))]
    seg_cols = [(p16, dk, c16["hq"], True), (p16, dk, c16["hi"], True), (p16, dk, c16["hg"], True),
                (p32, dk, c32["hf"], True), (p32, dk, c32["hf"] + n_heads * dk, True)]
    return _scan_call(functools.partial(_hgrn_kernel, chunk, rows.seq, rows.ctx, layer_j), "hgrn_scan",
                      rows, n_heads, dk, dk, chunk, param_specs, [lb_logits, gain], seg_cols)


def _qkv_kernel(n_q, n_kv, p_ref, cos_ref, sin_ref, qg_ref, kg_ref, q_ref, k_ref, v_ref):
    hd = ATTN_HEAD_DIM
    cos, sin = cos_ref[...], sin_ref[...]
    lane = lax.broadcasted_iota(jnp.int32, (1, hd), 1)
    first = (lane % (hd // 2)) < (hd // 4)

    def rope(a):
        rot = jnp.where(first, pltpu.roll(a, hd - hd // 4, axis=1), pltpu.roll(a, hd // 4, axis=1))
        return a * cos + rot * sin

    for h in range(n_q):
        a = _rms(p_ref[:, h * hd:(h + 1) * hd]) * qg_ref[...] * (hd ** -0.5 * LOG2E)
        q_ref[:, h * hd:(h + 1) * hd] = rope(a).astype(q_ref.dtype)
    for h in range(n_kv):
        a = _rms(p_ref[:, (n_q + h) * hd:(n_q + h + 1) * hd]) * kg_ref[...]
        k_ref[:, h * hd:(h + 1) * hd] = rope(a).astype(k_ref.dtype)
    v_ref[...] = p_ref[:, (n_q + n_kv) * hd:].astype(v_ref.dtype)


def _qkv_prep(p, rows, cos, sin, qg, kg, n_q, n_kv, tm):
    hd = ATTN_HEAD_DIM
    n_lat_tiles = rows.n_lat // tm
    per_seq = rows.seq // tm
    tab = pl.BlockSpec((tm, hd), lambda i: (jnp.where(i < n_lat_tiles, i % per_seq, per_seq), 0))
    return pl.pallas_call(
        functools.partial(_qkv_kernel, n_q, n_kv),
        grid=(rows.n_all // tm,),
        in_specs=[pl.BlockSpec((tm, p.shape[1]), lambda i: (i, 0)), tab, tab,
                  pl.BlockSpec((1, hd), lambda i: (0, 0)), pl.BlockSpec((1, hd), lambda i: (0, 0))],
        out_specs=[pl.BlockSpec((tm, n_q * hd), lambda i: (i, 0)),
                   pl.BlockSpec((tm, n_kv * hd), lambda i: (i, 0)),
                   pl.BlockSpec((tm, n_kv * hd), lambda i: (i, 0))],
        out_shape=[jax.ShapeDtypeStruct((rows.n_all, n_q * hd), MXU_DTYPE),
                   jax.ShapeDtypeStruct((rows.n_all, n_kv * hd), MXU_DTYPE),
                   jax.ShapeDtypeStruct((rows.n_all, n_kv * hd), MXU_DTYPE)],
        compiler_params=_cparams("parallel"),
        name="qkv_norm_rope",
    )(p, cos, sin, qg, kg)


def _attn_kernel(n_kv_sets, q_ref, *refs):
    kv = refs[:2 * n_kv_sets]
    o_ref = refs[2 * n_kv_sets]
    hd = ATTN_HEAD_DIM
    for g in range(ATTN_GROUP):
        q = q_ref[:, g * hd:(g + 1) * hd]
        s = [_dot_nt(q, kv[2 * i][...]) for i in range(n_kv_sets)]
        m = functools.reduce(jnp.maximum, [jnp.max(si, axis=-1, keepdims=True) for si in s])
        p = [jnp.exp2(si - m) for si in s]
        den = functools.reduce(jnp.add, [jnp.sum(pi, axis=-1, keepdims=True) for pi in p])
        acc = functools.reduce(jnp.add, [_dot(p[i].astype(MXU_DTYPE), kv[2 * i + 1][...]) for i in range(n_kv_sets)])
        o_ref[:, g * hd:(g + 1) * hd] = (acc / den).astype(o_ref.dtype)


def _attention(q, k, v, rows, n_kv, tq, ctx_queries):
    hd = ATTN_HEAD_DIM
    gw = ATTN_GROUP * hd
    ctx_base = rows.n_lat // rows.ctx
    ctx_kv = pl.BlockSpec((rows.ctx, hd), lambda b, h, i: (ctx_base + b, h))
    if ctx_queries:
        n_q, per = rows.batch * rows.ctx, 1
        tq = rows.ctx
        q_spec = pl.BlockSpec((tq, gw), lambda b, h, i: (ctx_base + b, h))
        kv_specs, kv_args = [ctx_kv, ctx_kv], [k, v]
    else:
        n_q, per = rows.n_lat, rows.seq // tq
        q_spec = pl.BlockSpec((tq, gw), lambda b, h, i: (b * per + i, h))
        lat_kv = pl.BlockSpec((rows.seq, hd), lambda b, h, i: (b, h))
        kv_specs, kv_args = [lat_kv, lat_kv, ctx_kv, ctx_kv], [k, v, k, v]
    return pl.pallas_call(
        functools.partial(_attn_kernel, len(kv_args) // 2),
        grid=(rows.batch, n_kv, per),
        in_specs=[q_spec] + kv_specs,
        out_specs=pl.BlockSpec((tq, gw), lambda b, h, i: (b * per + i, h)),
        out_shape=jax.ShapeDtypeStruct((n_q, n_kv * gw), MXU_DTYPE),
        compiler_params=_cparams("parallel", "parallel", "arbitrary"),
        name="gqa_ctx" if ctx_queries else "gqa_latent",
    )(q, *kv_args)


def _rope_tables(seq, pad_rows):
    hd = ATTN_HEAD_DIM
    half = hd // 2
    pos = jnp.arange(seq)
    inv_freq = ROPE_THETA ** (-jnp.arange(0, half, 2, dtype=F32) / half)
    ang = jnp.stack([pos // GRID_W, pos % GRID_W], axis=-1).astype(F32)[:, :, None] * inv_freq
    cos = jnp.concatenate([jnp.cos(ang), jnp.cos(ang)], axis=-1).reshape(seq, hd)
    sin = jnp.concatenate([-jnp.sin(ang), jnp.sin(ang)], axis=-1).reshape(seq, hd)
    cos = jnp.concatenate([cos, jnp.ones((pad_rows, hd), F32)], axis=0)
    sin = jnp.concatenate([sin, jnp.zeros((pad_rows, hd), F32)], axis=0)
    return cos, sin


def _final_kernel(x_ref, g_ref, o_ref):
    o_ref[...] = _rms(x_ref[...]) * g_ref[...]


def _final_norm(x, gain, tm):
    n, d = x.shape
    return pl.pallas_call(
        _final_kernel,
        grid=(n // tm,),
        in_specs=[pl.BlockSpec((tm, d), lambda i: (i, 0)), pl.BlockSpec((1, d), lambda i: (0, 0))],
        out_specs=pl.BlockSpec((tm, d), lambda i: (i, 0)),
        out_shape=jax.ShapeDtypeStruct((n, d), F32),
        compiler_params=_cparams("parallel"),
        name="final_norm",
    )(x, gain.reshape(1, d))


def _even_layout(d):
    gla_w = d // 2
    gla_qk = gla_w // 2
    hgrn_w = d // 2
    names = ["gq", "gk", "gv", "gr", "glr", "hq", "hf", "hi", "hg"]
    widths = [gla_qk, gla_qk, gla_w, gla_w, 2 * GLA_GATE_RANK, hgrn_w, 2 * hgrn_w, hgrn_w, hgrn_w]
    src = dict(zip(names, zip((int(o) for o in np.cumsum([0] + widths[:-1])), widths)))
    narrow = {k: src[k] for k in ("gq", "gk", "gv", "gr", "hq", "hi", "hg")}
    wide = {k: src[k] for k in ("glr", "hf")}
    return narrow, wide


def _gather_cols(w, group, pad_to):
    parts, offs, off = [], {}, 0
    for name, (c0, width) in group.items():
        padded = -(-width // pad_to) * pad_to
        parts.append(w[:, c0:c0 + width])
        if padded > width:
            parts.append(jnp.zeros((w.shape[0], padded - width), w.dtype))
        offs[name] = off
        off += padded
    return jnp.concatenate(parts, axis=1), offs


def kernel(x, c, ctx, c_ctx, w_mod, b_mod, ffn_w_gate, ffn_w_up, ffn_w_down, mix_ab_w_in, gla_gate_w2, gla_gate_b,
           gla_norm_g, hgrn_lb_logits, hgrn_norm_g, mix_ab_w_out, attn_w_in, attn_q_norm_g, attn_k_norm_g,
           attn_w_out, final_norm_g):
    batch, seq, d = x.shape
    n_ctx = ctx.shape[1]
    depth = w_mod.shape[0]
    rows = _Rows(batch, seq, n_ctx)
    tm = _row_tile(seq, batch * n_ctx, 512)
    tm_mm = _row_tile(seq, batch * n_ctx, 1024)
    bf = MXU_DTYPE

    gla_w = d // 2
    gla_dv = gla_w // GLA_HEADS
    gla_dk = gla_dv // 2
    hgrn_w = d // 2
    hgrn_heads = hgrn_w // HGRN_EXPAND
    n_q_heads = d // ATTN_HEAD_DIM
    n_kv_heads = n_q_heads // ATTN_GROUP

    cond = jnp.zeros((MOD_ROWS, d), F32).at[:batch].set(c).at[batch].set(c_ctx)
    mod = _modulation(cond, w_mod, b_mod)
    mod = mod.reshape(depth, MOD_ROWS, N_MOD, d).transpose(0, 2, 1, 3).reshape(depth, N_MOD * MOD_ROWS, 1, d)

    narrow, wide = _even_layout(d)
    cos, sin = _rope_tables(seq, tm)

    xs = jnp.concatenate([x.reshape(batch * seq, d), ctx.reshape(batch * n_ctx, d)], axis=0)
    for layer in range(depth):
        last = layer == depth - 1
        j = layer // 2
        m = mod[layer]
        xs = _ffn(xs, rows.n_all, rows, m, (0, 1, 2), ffn_w_gate[layer, 0].astype(bf), ffn_w_up[layer, 0].astype(bf),
                  ffn_w_down[layer, 0].astype(bf), tm_mm)
        if layer % 2 == 0:
            w16, c16 = _gather_cols(mix_ab_w_in[j], narrow, MXU_N)
            w32, c32 = _gather_cols(mix_ab_w_in[j], wide, MXU_N)
            p16 = _proj(xs, rows, m, (3, 4), w16.astype(bf), tm_mm, _col_tile(w16.shape[1]), bf)
            p32 = _proj(xs, rows, m, (3, 4), w32.astype(bf), tm_mm, _col_tile(w32.shape[1]), F32)
            w2p = jnp.zeros((2, LANES, gla_w // 2), F32)
            for direction in range(2):
                r0 = direction * GLA_GATE_RANK
                w2p = w2p.at[direction, r0:r0 + GLA_GATE_RANK].set(gla_gate_w2[j, direction])
            ya_l, ya_c = _gla_mixer(p16, p32, rows, c16, c32, w2p.astype(bf), gla_gate_b[j].reshape(2, 1, -1),
                                    gla_norm_g[j].reshape(1, -1), gla_dk, gla_dv)
            yb_l, yb_c = _hgrn_mixer(p16, p32, rows, c16, c32, hgrn_lb_logits, hgrn_norm_g[j].reshape(1, -1), j,
                                     hgrn_heads, HGRN_EXPAND)
            lat_parts, ctx_parts = [ya_l, yb_l], [ya_c, yb_c]
            w_out = mix_ab_w_out[j].astype(bf)
        else:
            p = _proj(xs, rows, m, (3, 4), attn_w_in[j].astype(bf), tm_mm, _col_tile(attn_w_in.shape[2]))
            q, k, v = _qkv_prep(p, rows, cos, sin, attn_q_norm_g[j].reshape(1, -1), attn_k_norm_g[j].reshape(1, -1),
                                n_q_heads, n_kv_heads, tm)
            lat_parts = [_attention(q, k, v, rows, n_kv_heads, _row_tile(seq, seq, 256), False)]
            ctx_parts = None if last else [_attention(q, k, v, rows, n_kv_heads, n_ctx, True)]
            w_out = attn_w_out[j].astype(bf)
        if last:
            ctx_parts = None
        xs = _out_proj(xs, rows, m, 5, w_out, lat_parts, ctx_parts, tm)
        n_rows = rows.n_lat if last else rows.n_all
        xs = _ffn(xs, n_rows, rows, m, (6, 7, 8), ffn_w_gate[layer, 1].astype(bf), ffn_w_up[layer, 1].astype(bf),
                  ffn_w_down[layer, 1].astype(bf), tm_mm)
    out = _final_norm(xs[:rows.n_lat], final_norm_g, tm)
    return out.reshape(batch, seq, d)
```

```python
import functools

import jax
import jax.numpy as jnp
import numpy as np
from jax import lax
from jax.experimental import pallas as pl
from jax.experimental.pallas import tpu as pltpu

F32 = jnp.float32
MXU_DTYPE = jnp.bfloat16

EPS = 1e-6
F_FLOOR = 1e-30
N_MOD = 9
GRID_W = 64
GLA_HEADS = 4
GLA_GATE_RANK = 16
GLA_GATE_TAU = 16.0
HGRN_EXPAND = 128
ATTN_HEAD_DIM = 128
ATTN_GROUP = 4
ATTN_KV_PER_STEP = 2
ROPE_THETA = 10000.0

LANES = 128
MXU_N = 256
VMEM_LIMIT_BYTES = 56 * 1024 * 1024

SCAN_CHUNK = 256
SUBLANES = 8
LOG2E = 1.4426950408889634
MOD_ROWS = 16


def _cparams(*sem):
    return pltpu.CompilerParams(dimension_semantics=sem, vmem_limit_bytes=VMEM_LIMIT_BYTES)


def _sigmoid(x):
    return 1.0 / (1.0 + jnp.exp(-x))


def _silu(x):
    return x * _sigmoid(x)


def _rms(x):
    return x * lax.rsqrt(jnp.mean(x * x, axis=-1, keepdims=True) + EPS)


def _dot(a, b):
    return jnp.dot(a, b, preferred_element_type=F32)


def _dot_nt(a, b):
    return lax.dot_general(a, b, (((1,), (1,)), ((), ())), preferred_element_type=F32)


def _dot_tn(a, b):
    return lax.dot_general(a, b, (((0,), (0,)), ((), ())), preferred_element_type=F32)


def _row_tile(n_lat, n_ctx_total, cap):
    t = cap
    while n_lat % t or n_ctx_total % t:
        t //= 2
    return t


def _col_tile(n, cap=1024):
    return max(t for t in range(MXU_N, cap + 1, MXU_N) if n % t == 0)


def _mod_kernel(c_ref, w_ref, b_ref, o_ref):
    s = _silu(c_ref[...]).astype(MXU_DTYPE)
    o_ref[...] = _dot(s, w_ref[...].astype(MXU_DTYPE)) + b_ref[...]


def _modulation(cond, w_mod, b_mod):
    depth, d, n = w_mod.shape
    tn = 1024
    return pl.pallas_call(
        _mod_kernel,
        grid=(depth, n // tn),
        in_specs=[pl.BlockSpec((MOD_ROWS, d), lambda l, j: (0, 0)),
                  pl.BlockSpec((None, d, tn), lambda l, j: (l, 0, j)),
                  pl.BlockSpec((None, 1, tn), lambda l, j: (l, 0, j))],
        out_specs=pl.BlockSpec((None, MOD_ROWS, tn), lambda l, j: (l, 0, j)),
        out_shape=jax.ShapeDtypeStruct((depth, MOD_ROWS, n), F32),
        compiler_params=_cparams("parallel", "parallel"),
        name="modulation",
    )(cond, w_mod, b_mod.reshape(depth, 1, n))


class _Rows:
    def __init__(self, batch, seq, ctx):
        self.batch, self.seq, self.ctx = batch, seq, ctx
        self.n_lat = batch * seq
        self.n_all = self.n_lat + batch * ctx

    def mod_spec(self, tm, width, slot, by_column=False):
        seq, batch = self.seq, self.batch

        def row(i):
            return slot * MOD_ROWS + jnp.minimum((i * tm) // seq, batch)
        if by_column:
            return pl.BlockSpec((None, 1, width), lambda i, j: (row(i), 0, j))
        return pl.BlockSpec((None, 1, width), lambda i, *_: (row(i), 0, 0))


def _ffn_up_kernel(x_ref, sh_ref, sc_ref, wg_ref, wu_ref, o_ref, h_scr):
    @pl.when(pl.program_id(1) == 0)
    def _():
        h = _rms(x_ref[...]) * (1.0 + sc_ref[...]) + sh_ref[...]
        h_scr[...] = h.astype(MXU_DTYPE)

    h = h_scr[...]
    o_ref[...] = (_silu(_dot(h, wg_ref[...])) * _dot(h, wu_ref[...])).astype(o_ref.dtype)


def _ffn_down_kernel(a_ref, wd_ref, x_ref, gt_ref, o_ref):
    o_ref[...] = x_ref[...] + 0.5 * gt_ref[...] * _dot(a_ref[...], wd_ref[...])


def _ffn(x, n_rows, rows, mod, slots, wg, wu, wd, which, tm):
    d = x.shape[1]
    dff = wg.shape[3]
    layer, half = which
    tf = _col_tile(dff, 512)
    act = pl.pallas_call(
        _ffn_up_kernel,
        grid=(n_rows // tm, dff // tf),
        in_specs=[pl.BlockSpec((tm, d), lambda i, j: (i, 0)),
                  rows.mod_spec(tm, d, slots[0]), rows.mod_spec(tm, d, slots[1]),
                  pl.BlockSpec((None, None, d, tf), lambda i, j: (layer, half, 0, j)),
                  pl.BlockSpec((None, None, d, tf), lambda i, j: (layer, half, 0, j))],
        out_specs=pl.BlockSpec((tm, tf), lambda i, j: (i, j)),
        out_shape=jax.ShapeDtypeStruct((n_rows, dff), MXU_DTYPE),
        scratch_shapes=[pltpu.VMEM((tm, d), MXU_DTYPE)],
        compiler_params=_cparams("parallel", "arbitrary"),
        name="ffn_up",
    )(x, mod, mod, wg, wu)
    tn = _col_tile(d, 512)
    return pl.pallas_call(
        _ffn_down_kernel,
        grid=(n_rows // tm, d // tn),
        in_specs=[pl.BlockSpec((tm, dff), lambda i, j: (i, 0)),
                  pl.BlockSpec((None, None, dff, tn), lambda i, j: (layer, half, 0, j)),
                  pl.BlockSpec((tm, tn), lambda i, j: (i, j)),
                  rows.mod_spec(tm, tn, slots[2], by_column=True)],
        out_specs=pl.BlockSpec((tm, tn), lambda i, j: (i, j)),
        out_shape=jax.ShapeDtypeStruct((n_rows, d), F32),
        compiler_params=_cparams("parallel", "arbitrary"),
        name="ffn_down",
    )(act, wd, x, mod)


def _proj_kernel(x_ref, sh_ref, sc_ref, w_ref, o_ref, h_scr):
    @pl.when(pl.program_id(1) == 0)
    def _():
        h = _rms(x_ref[...]) * (1.0 + sc_ref[...]) + sh_ref[...]
        h_scr[...] = h.astype(MXU_DTYPE)

    o_ref[...] = _dot(h_scr[...], w_ref[...]).astype(o_ref.dtype)


def _proj(x, rows, mod, slots, w, tm, tn, out_dtype=F32):
    d = x.shape[1]
    n = w.shape[1]
    return pl.pallas_call(
        _proj_kernel,
        grid=(rows.n_all // tm, n // tn),
        in_specs=[pl.BlockSpec((tm, d), lambda i, j: (i, 0)),
                  rows.mod_spec(tm, d, slots[0]), rows.mod_spec(tm, d, slots[1]),
                  pl.BlockSpec((d, tn), lambda i, j: (0, j))],
        out_specs=pl.BlockSpec((tm, tn), lambda i, j: (i, j)),
        out_shape=jax.ShapeDtypeStruct((rows.n_all, n), out_dtype),
        scratch_shapes=[pltpu.VMEM((tm, d), MXU_DTYPE)],
        compiler_params=_cparams("parallel", "arbitrary"),
        name="mixer_in_proj",
    )(x, mod, mod, w)


def _out_kernel(n_parts, widths, n_lat_tiles, has_ctx, *refs):
    x_ref, gt_ref, w_ref = refs[0], refs[1], refs[2]
    lat = refs[3:3 + n_parts]
    ctx = refs[3 + n_parts:3 + 2 * n_parts] if has_ctx else ()
    o_ref = refs[-1]

    def run(parts):
        acc = None
        off = 0
        for p, wd in zip(parts, widths):
            t = _dot(p[...], w_ref[off:off + wd, :])
            acc = t if acc is None else acc + t
            off += wd
        o_ref[...] = x_ref[...] + gt_ref[...] * acc

    if has_ctx:
        i = pl.program_id(0)
        pl.when(i < n_lat_tiles)(lambda: run(lat))
        pl.when(i >= n_lat_tiles)(lambda: run(ctx))
    else:
        run(lat)


def _out_proj(x, rows, mod, slot, w, lat_parts, ctx_parts, tm):
    d = x.shape[1]
    has_ctx = ctx_parts is not None
    n_rows = rows.n_all if has_ctx else rows.n_lat
    n_lat_tiles = rows.n_lat // tm
    widths = tuple(p.shape[1] for p in lat_parts)
    specs = [pl.BlockSpec((tm, d), lambda i: (i, 0)), rows.mod_spec(tm, d, slot),
             pl.BlockSpec(w.shape, lambda i: (0, 0))]
    specs += [pl.BlockSpec((tm, wd), lambda i: (jnp.minimum(i, n_lat_tiles - 1), 0)) for wd in widths]
    args = [x, mod, w] + list(lat_parts)
    if has_ctx:
        specs += [pl.BlockSpec((tm, wd), lambda i: (jnp.maximum(i - n_lat_tiles, 0), 0)) for wd in widths]
        args += list(ctx_parts)
    return pl.pallas_call(
        functools.partial(_out_kernel, len(widths), widths, n_lat_tiles, has_ctx),
        grid=(n_rows // tm,),
        in_specs=specs,
        out_specs=pl.BlockSpec((tm, d), lambda i: (i, 0)),
        out_shape=jax.ShapeDtypeStruct((n_rows, d), F32),
        compiler_params=_cparams("parallel"),
        name="mixer_out_proj",
    )(*args)


def _scan_tables(chunk):
    c = chunk
    t = np.arange(c)[:, None]
    u = np.arange(c)[None, :]
    tri = (u <= t).astype(np.float32)
    x = t ^ u
    lvl = np.zeros((c, c), np.int32)
    nz = x > 0
    lvl[nz] = int(np.log2(c)) - np.floor(np.log2(x[nz])).astype(np.int32)
    return tri, lvl[:c // 2, :c // 2]


def _level_ref(cum, m):
    c, dk = cum.shape
    if 2 * m >= SUBLANES:
        x = cum.reshape(c // (2 * m), 2 * m, dk)
        return jnp.broadcast_to(x[:, m - 1:m, :], x.shape).reshape(c, dk)
    x = cum.reshape(c // SUBLANES, SUBLANES, dk)
    sub = lax.broadcasted_iota(jnp.int32, x.shape, 1)
    lo, hi = (jnp.broadcast_to(x[:, r:r + 1, :], x.shape) for r in (m - 1, 2 * m + m - 1))
    return jnp.where(sub < 2 * m, lo, hi).reshape(c, dk)


def _chunk_local(q, ks, vb, gs, tri, lvl):
    c, dk = q.shape
    n_levels = int(np.log2(c))
    g_f, g_b = (g * LOG2E for g in gs)
    pieces = []
    for g in (g_f, g_b):
        hi = g.astype(MXU_DTYPE)
        pieces += [hi, (g - hi.astype(F32)).astype(MXU_DTYPE)]
    ex = _dot(tri, jnp.concatenate(pieces, axis=1))
    cum_f = ex[:, :dk] + ex[:, dk:2 * dk]
    cum_b = ex[:, 2 * dk:3 * dk] + ex[:, 3 * dk:]
    tot_f, tot_b = cum_f[c - 1:c], cum_b[c - 1:c]
    a_b = cum_b - g_b
    same_k = ks[0] is ks[1]
    k_sum = ks[0] * 2.0 if same_k else ks[0] + ks[1]
    h = c // 2

    def diag_blocks(zq, zk):
        zq, zk = zq.astype(MXU_DTYPE), zk.astype(MXU_DTYPE)
        return [_dot_nt(zq[:h], zk[:h]), _dot_nt(zq[h:], zk[h:])]

    attn = [jnp.where(lvl == 0, a, 0.0) for a in diag_blocks(q, k_sum)]
    row = lax.broadcasted_iota(jnp.int32, (c, 1), 0)
    for li in range(n_levels):
        m = c >> (li + 1)
        second = (row & m) != 0
        k_m = ks[0] if same_k else jnp.where(second, ks[1], ks[0])
        if m == 1:
            zq = q * jnp.exp2(jnp.where(second, g_f, g_b))
            zk = k_m
        else:
            d_f = cum_f - _level_ref(cum_f, m)
            d_b = a_b - _level_ref(cum_b, m)
            zq = q * jnp.exp2(jnp.where(second, d_f, -d_b))
            zk = k_m * jnp.exp2(jnp.where(second, d_b, -d_f))
        if m == h:
            zq, zk = zq.astype(MXU_DTYPE), zk.astype(MXU_DTYPE)
            lower, upper = _dot_nt(zq[h:], zk[:h]), _dot_nt(zq[:h], zk[h:])
        else:
            attn = [jnp.where(lvl == li + 1, a, old) for a, old in zip(diag_blocks(zq, zk), attn)]
    attn = jnp.concatenate([jnp.concatenate([attn[0], upper], axis=1),
                            jnp.concatenate([lower, attn[1]], axis=1)], axis=0)
    o = _dot(attn.astype(MXU_DTYPE), vb)
    fwd = ((q * jnp.exp2(cum_f)).astype(MXU_DTYPE), (ks[0] * jnp.exp2(tot_f - cum_f)).astype(MXU_DTYPE),
           jnp.exp2(tot_f))
    bwd = ((q * jnp.exp2(tot_b - a_b)).astype(MXU_DTYPE), (ks[1] * jnp.exp2(a_b)).astype(MXU_DTYPE),
           jnp.exp2(tot_b))
    return o, fwd, bwd


def _bidir_scan(segments, chunk, gates, finish, tabs, scr):
    tri, lvl = tabs
    acc, vt_s, qh_s, kh_s, dec_s, st_s = scr
    n_seg = [n_rows // chunk for n_rows, _ in segments]
    n_all = sum(n_seg)

    def rows_of(j):
        return pl.ds(pl.multiple_of(j * chunk, chunk), chunk)

    base = 0
    for (n_rows, seg), n in zip(segments, n_seg):
        def local(ci, _, seg=seg, base=base):
            q, ks, v, gs = gates(seg, rows_of(ci))
            j = base + ci
            vb = v.astype(MXU_DTYPE)
            vt_s[j] = vb.T
            o, fwd, bwd = _chunk_local(q, ks, vb, gs, tri, lvl)
            for d, (qh, kh, dec) in enumerate((fwd, bwd)):
                qh_s[d, rows_of(j), :] = qh
                kh_s[d, rows_of(j), :] = kh
                dec_s[d, pl.ds(pl.multiple_of(j * SUBLANES, SUBLANES), SUBLANES), :] = jnp.broadcast_to(
                    dec, (SUBLANES, dec.shape[1]))
            acc[rows_of(j), :] = o
            return 0
        lax.fori_loop(0, n, local, 0)
        base += n

    st_s[...] = jnp.zeros_like(st_s)
    n_first = n_seg[0]

    def carry(i, _):
        j_b = jnp.where(i < n_first, n_first - 1 - i, n_all + n_first - 1 - i)
        for d, j in enumerate((i, j_b)):
            st = st_s[d]
            acc[rows_of(j), :] += _dot_nt(qh_s[d, rows_of(j), :], st.astype(MXU_DTYPE))
            dec = dec_s[d, pl.ds(pl.multiple_of(j * SUBLANES, SUBLANES), 1), :]
            st_s[d] = st * dec + _dot(vt_s[j], kh_s[d, rows_of(j), :])
        return 0
    lax.fori_loop(0, n_all, carry, 0, unroll=True)

    base = 0
    for (n_rows, seg), n in zip(segments, n_seg):
        def done(ci, _, seg=seg, base=base):
            finish(seg, rows_of(ci), acc[rows_of(base + ci), :])
            return 0
        lax.fori_loop(0, n, done, 0)
        base += n


def _log_sigmoid(x):
    return jnp.minimum(x, 0.0) - jnp.log(1.0 + jnp.exp(-jnp.abs(x)))


def _gla_kernel(chunk, seq, ctx, scale,
                tri, lvl, w2_ref, b_ref, gain_ref,
                q_c, k_c, v_c, gr_c, lr_c, q_l, k_l, v_l, gr_l, lr_l,
                y_c, y_l, *scr):
    ins = ((q_c, k_c, v_c, lr_c, gr_c, y_c), (q_l, k_l, v_l, lr_l, gr_l, y_l))

    def gates(seg, sl):
        q_r, k_r, v_r, lr_r = ins[seg][:4]
        lr = lr_r[sl, :].astype(MXU_DTYPE)
        gs = tuple(_log_sigmoid(_dot(lr, w2_ref[d]) + b_ref[d]) * (1.0 / GLA_GATE_TAU) for d in range(2))
        k = k_r[sl, :].astype(F32)
        return q_r[sl, :].astype(F32) * scale, (k, k), v_r[sl, :], gs

    def finish(seg, sl, o):
        gr_r, y_r = ins[seg][4:]
        y_r[sl, :] = (_rms(o) * gain_ref[...] * _silu(gr_r[sl, :].astype(F32))).astype(y_r.dtype)

    _bidir_scan([(ctx, 0), (seq, 1)], chunk, gates, finish, (tri[...], lvl[...]), scr)


def _hgrn_kernel(chunk, seq, ctx, layer_j,
                 tri, lvl, lbl_ref, gain_ref,
                 q_c, i_c, hg_c, ff_c, fb_c, q_l, i_l, hg_l, ff_l, fb_l,
                 y_c, y_l, *scr):
    ins = ((q_c, (ff_c, fb_c), i_c, hg_c, y_c), (q_l, (ff_l, fb_l), i_l, hg_l, y_l))
    lbs = []
    for direction in range(2):
        logits = lbl_ref[direction]
        pe = jnp.exp(logits - jnp.max(logits, axis=0, keepdims=True))
        p = pe / jnp.sum(pe, axis=0, keepdims=True)
        lbs.append(jnp.sum(p[0:layer_j + 1], axis=0, keepdims=True) - p[0:1])

    def gates(seg, sl):
        q_r, f_rs, i_r = ins[seg][:3]
        ks, gs = [], []
        for d in range(2):
            lb = lbs[d]
            x = f_rs[d][sl, :]
            en = jnp.exp(-jnp.abs(x))
            inv = 1.0 / (1.0 + en)
            pos = x >= 0.0
            sig_p = jnp.where(pos, inv, en * inv)
            sig_n = jnp.where(pos, en * inv, inv)
            gs.append(jnp.log(jnp.maximum(lb + (1.0 - lb) * sig_p, F_FLOOR)))
            ks.append((1.0 - lb) * sig_n)
        return _silu(q_r[sl, :].astype(F32)), ks, i_r[sl, :], gs

    def finish(seg, sl, o):
        hg_r, y_r = ins[seg][3:]
        y_r[sl, :] = (_rms(o) * gain_ref[...] * _silu(hg_r[sl, :].astype(F32))).astype(y_r.dtype)

    _bidir_scan([(ctx, 0), (seq, 1)], chunk, gates, finish, (tri[...], lvl[...]), scr)


def _scan_scratch(chunk, n_rows, dk, dv):
    n_chunks = n_rows // chunk
    return [pltpu.VMEM((n_rows, dv), F32), pltpu.VMEM((n_chunks, dv, chunk), MXU_DTYPE),
            pltpu.VMEM((2, n_rows, dk), MXU_DTYPE), pltpu.VMEM((2, n_rows, dk), MXU_DTYPE),
            pltpu.VMEM((2, n_chunks * SUBLANES, dk), F32), pltpu.VMEM((2, dv, dk), F32)]


def _scan_chunk(rows):
    return _row_tile(rows.seq, rows.ctx, SCAN_CHUNK)


def _scan_table_args(chunk):
    tri, lvl = _scan_tables(chunk)
    args = [jnp.asarray(tri, MXU_DTYPE), jnp.asarray(lvl)]
    specs = [pl.BlockSpec(a.shape, lambda b, h: (0, 0)) for a in args]
    return args, specs


def _seg_inputs(rows, cols):
    ctx_base = rows.n_lat // rows.ctx
    specs, arrays = [], []
    for n_rows, base in ((rows.ctx, ctx_base), (rows.seq, 0)):
        for arr, w, c0, per_head in cols:
            specs.append(pl.BlockSpec(
                (n_rows, w), lambda b, h, base=base, cb=c0 // w, ph=int(per_head): (base + b, cb + ph * h)))
            arrays.append(arr)
    return specs, arrays


def _scan_call(body, name, rows, n_heads, dk, dv, chunk, param_specs, params, seg_cols):
    targs, tspecs = _scan_table_args(chunk)
    seg_specs, seg_arrays = _seg_inputs(rows, seg_cols)
    width = n_heads * dv
    y_c, y_l = pl.pallas_call(
        body,
        grid=(rows.batch, n_heads),
        in_specs=tspecs + param_specs + seg_specs,
        out_specs=[pl.BlockSpec((rows.ctx, dv), lambda b, h: (b, h)),
                   pl.BlockSpec((rows.seq, dv), lambda b, h: (b, h))],
        out_shape=[jax.ShapeDtypeStruct((rows.batch * rows.ctx, width), MXU_DTYPE),
                   jax.ShapeDtypeStruct((rows.n_lat, width), MXU_DTYPE)],
        scratch_shapes=_scan_scratch(chunk, rows.ctx + rows.seq, dk, dv),
        compiler_params=_cparams("parallel", "parallel"),
        name=name,
    )(*targs, *params, *seg_arrays)
    return y_l, y_c


def _gla_mixer(p16, p32, rows, c16, c32, w2p, b2, gain, dk, dv):
    chunk = _scan_chunk(rows)
    u = LANES
    param_specs = [pl.BlockSpec((2, u, dk), lambda b, h: (0, 0, h)),
                   pl.BlockSpec((2, 1, dk), lambda b, h: (0, 0, h)),
                   pl.BlockSpec((1, dv), lambda b, h: (0, 0))]
    seg_cols = [(p16, dk, c16["gq"], True), (p16, dk, c16["gk"], True), (p16, dv, c16["gv"], True),
                (p16, dv, c16["gr"], True), (p32, u, c32["glr"], False)]
    return _scan_call(functools.partial(_gla_kernel, chunk, rows.seq, rows.ctx, dk ** -0.5), "gla_scan",
                      rows, GLA_HEADS, dk, dv, chunk, param_specs, [w2p, b2, gain], seg_cols)


def _hgrn_mixer(p16, p32, rows, c16, c32, lb_logits, gain, layer_j, n_heads, dk):
    chunk = _scan_chunk(rows)
    n_even = lb_logits.shape[1]
    param_specs = [pl.BlockSpec((2, n_even, dk), lambda b, h: (0, 0, h)),
                   pl.BlockSpec((1, dk), lambda b, h: (0, 0))]
    seg_cols = [(p16, dk, c16["hq"], True), (p16, dk, c16["hi"], True), (p16, dk, c16["hg"], True),
                (p32, dk, c32["hf"], True), (p32, dk, c32["hf"] + n_heads * dk, True)]
    return _scan_call(functools.partial(_hgrn_kernel, chunk, rows.seq, rows.ctx, layer_j), "hgrn_scan",
                      rows, n_heads, dk, dk, chunk, param_specs, [lb_logits, gain], seg_cols)


def _qkv_kernel(n_q, n_kv, p_ref, cos_ref, sin_ref, qg_ref, kg_ref, q_ref, k_ref, v_ref):
    hd = ATTN_HEAD_DIM
    cos, sin = cos_ref[...], sin_ref[...]
    lane = lax.broadcasted_iota(jnp.int32, (1, hd), 1)
    first = (lane % (hd // 2)) < (hd // 4)

    def rope(a):
        rot = jnp.where(first, pltpu.roll(a, hd - hd // 4, axis=1), pltpu.roll(a, hd // 4, axis=1))
        return a * cos + rot * sin

    for h in range(n_q):
        a = _rms(p_ref[:, h * hd:(h + 1) * hd]) * qg_ref[...] * (hd ** -0.5 * LOG2E)
        q_ref[:, h * hd:(h + 1) * hd] = rope(a).astype(q_ref.dtype)
    for h in range(n_kv):
        a = _rms(p_ref[:, (n_q + h) * hd:(n_q + h + 1) * hd]) * kg_ref[...]
        k_ref[:, h * hd:(h + 1) * hd] = rope(a).astype(k_ref.dtype)
    v_ref[...] = p_ref[:, (n_q + n_kv) * hd:].astype(v_ref.dtype)


def _qkv_prep(p, rows, cos, sin, qg, kg, n_q, n_kv, tm):
    hd = ATTN_HEAD_DIM
    n_lat_tiles = rows.n_lat // tm
    per_seq = rows.seq // tm
    tab = pl.BlockSpec((tm, hd), lambda i: (jnp.where(i < n_lat_tiles, i % per_seq, per_seq), 0))
    return pl.pallas_call(
        functools.partial(_qkv_kernel, n_q, n_kv),
        grid=(rows.n_all // tm,),
        in_specs=[pl.BlockSpec((tm, p.shape[1]), lambda i: (i, 0)), tab, tab,
                  pl.BlockSpec((1, hd), lambda i: (0, 0)), pl.BlockSpec((1, hd), lambda i: (0, 0))],
        out_specs=[pl.BlockSpec((tm, n_q * hd), lambda i: (i, 0)),
                   pl.BlockSpec((tm, n_kv * hd), lambda i: (i, 0)),
                   pl.BlockSpec((tm, n_kv * hd), lambda i: (i, 0))],
        out_shape=[jax.ShapeDtypeStruct((rows.n_all, n_q * hd), MXU_DTYPE),
                   jax.ShapeDtypeStruct((rows.n_all, n_kv * hd), MXU_DTYPE),
                   jax.ShapeDtypeStruct((rows.n_all, n_kv * hd), MXU_DTYPE)],
        compiler_params=_cparams("parallel"),
        name="qkv_norm_rope",
    )(p, cos, sin, qg, kg)


def _attn_kernel(n_kv_sets, kv_heads, q_ref, *refs):
    kv = refs[:2 * n_kv_sets]
    o_ref, s_scr = refs[2 * n_kv_sets:]
    hd = ATTN_HEAD_DIM
    n_heads = kv_heads * ATTN_GROUP
    bounds = np.cumsum([0] + [kv[2 * i].shape[0] for i in range(n_kv_sets)])

    def scores(g):
        q = q_ref[:, g * hd:(g + 1) * hd]
        c0 = (g // ATTN_GROUP) * hd
        for i in range(n_kv_sets):
            s_scr[g % 2, :, bounds[i]:bounds[i + 1]] = _dot_nt(q, kv[2 * i][:, c0:c0 + hd])

    scores(0)
    for g in range(n_heads):
        if g + 1 < n_heads:
            scores(g + 1)
        s = s_scr[g % 2]
        p = jnp.exp2(s - jnp.max(s, axis=-1, keepdims=True))
        den = jnp.sum(p, axis=-1, keepdims=True)
        pb = p.astype(MXU_DTYPE)
        c0 = (g // ATTN_GROUP) * hd
        acc = functools.reduce(jnp.add, [_dot(pb[:, bounds[i]:bounds[i + 1]], kv[2 * i + 1][:, c0:c0 + hd])
                                         for i in range(n_kv_sets)])
        o_ref[:, g * hd:(g + 1) * hd] = (acc / den).astype(o_ref.dtype)


def _attention(q, k, v, rows, n_kv, tq, ctx_queries):
    hd = ATTN_HEAD_DIM
    kvh = ATTN_KV_PER_STEP if n_kv % ATTN_KV_PER_STEP == 0 else 1
    kw = kvh * hd
    gw = ATTN_GROUP * kw
    ctx_base = rows.n_lat // rows.ctx
    ctx_kv = pl.BlockSpec((rows.ctx, kw), lambda b, h, i: (ctx_base + b, h))
    if ctx_queries:
        n_q, per = rows.batch * rows.ctx, 1
        tq = rows.ctx
        q_spec = pl.BlockSpec((tq, gw), lambda b, h, i: (ctx_base + b, h))
        kv_specs, kv_args, n_keys = [ctx_kv, ctx_kv], [k, v], rows.ctx
    else:
        n_q, per = rows.n_lat, rows.seq // tq
        q_spec = pl.BlockSpec((tq, gw), lambda b, h, i: (b * per + i, h))
        lat_kv = pl.BlockSpec((rows.seq, kw), lambda b, h, i: (b, h))
        kv_specs, kv_args, n_keys = [lat_kv, lat_kv, ctx_kv, ctx_kv], [k, v, k, v], rows.seq + rows.ctx
    return pl.pallas_call(
        functools.partial(_attn_kernel, len(kv_args) // 2, kvh),
        grid=(rows.batch, n_kv // kvh, per),
        in_specs=[q_spec] + kv_specs,
        out_specs=pl.BlockSpec((tq, gw), lambda b, h, i: (b * per + i, h)),
        out_shape=jax.ShapeDtypeStruct((n_q, n_kv * ATTN_GROUP * hd), MXU_DTYPE),
        scratch_shapes=[pltpu.VMEM((2, tq, n_keys), F32)],
        compiler_params=_cparams("parallel", "parallel", "arbitrary"),
        name="gqa_ctx" if ctx_queries else "gqa_latent",
    )(q, *kv_args)


def _rope_tables(seq, pad_rows):
    hd = ATTN_HEAD_DIM
    half = hd // 2
    pos = jnp.arange(seq)
    inv_freq = ROPE_THETA ** (-jnp.arange(0, half, 2, dtype=F32) / half)
    ang = jnp.stack([pos // GRID_W, pos % GRID_W], axis=-1).astype(F32)[:, :, None] * inv_freq
    cos = jnp.concatenate([jnp.cos(ang), jnp.cos(ang)], axis=-1).reshape(seq, hd)
    sin = jnp.concatenate([-jnp.sin(ang), jnp.sin(ang)], axis=-1).reshape(seq, hd)
    cos = jnp.concatenate([cos, jnp.ones((pad_rows, hd), F32)], axis=0)
    sin = jnp.concatenate([sin, jnp.zeros((pad_rows, hd), F32)], axis=0)
    return cos, sin


def _final_kernel(x_ref, g_ref, o_ref):
    o_ref[...] = _rms(x_ref[...]) * g_ref[...]


def _final_norm(x, gain, tm):
    n, d = x.shape
    return pl.pallas_call(
        _final_kernel,
        grid=(n // tm,),
        in_specs=[pl.BlockSpec((tm, d), lambda i: (i, 0)), pl.BlockSpec((1, d), lambda i: (0, 0))],
        out_specs=pl.BlockSpec((tm, d), lambda i: (i, 0)),
        out_shape=jax.ShapeDtypeStruct((n, d), F32),
        compiler_params=_cparams("parallel"),
        name="final_norm",
    )(x, gain.reshape(1, d))


def _even_layout(d):
    gla_w = d // 2
    gla_qk = gla_w // 2
    hgrn_w = d // 2
    names = ["gq", "gk", "gv", "gr", "glr", "hq", "hf", "hi", "hg"]
    widths = [gla_qk, gla_qk, gla_w, gla_w, 2 * GLA_GATE_RANK, hgrn_w, 2 * hgrn_w, hgrn_w, hgrn_w]
    src = dict(zip(names, zip((int(o) for o in np.cumsum([0] + widths[:-1])), widths)))
    narrow = {k: src[k] for k in ("gq", "gk", "gv", "gr", "hq", "hi", "hg")}
    wide = {k: src[k] for k in ("glr", "hf")}
    return narrow, wide


def _gather_cols(w, group, pad_to):
    parts, offs, off = [], {}, 0
    for name, (c0, width) in group.items():
        padded = -(-width // pad_to) * pad_to
        parts.append(w[:, c0:c0 + width])
        if padded > width:
            parts.append(jnp.zeros((w.shape[0], padded - width), w.dtype))
        offs[name] = off
        off += padded
    return jnp.concatenate(parts, axis=1), offs


def kernel(x, c, ctx, c_ctx, w_mod, b_mod, ffn_w_gate, ffn_w_up, ffn_w_down, mix_ab_w_in, gla_gate_w2, gla_gate_b,
           gla_norm_g, hgrn_lb_logits, hgrn_norm_g, mix_ab_w_out, attn_w_in, attn_q_norm_g, attn_k_norm_g,
           attn_w_out, final_norm_g):
    batch, seq, d = x.shape
    n_ctx = ctx.shape[1]
    depth = w_mod.shape[0]
    rows = _Rows(batch, seq, n_ctx)
    tm = _row_tile(seq, batch * n_ctx, 512)
    tm_mm = _row_tile(seq, batch * n_ctx, 1024)
    bf = MXU_DTYPE
    wg_all, wu_all, wd_all = ffn_w_gate.astype(bf), ffn_w_up.astype(bf), ffn_w_down.astype(bf)

    gla_w = d // 2
    gla_dv = gla_w // GLA_HEADS
    gla_dk = gla_dv // 2
    hgrn_w = d // 2
    hgrn_heads = hgrn_w // HGRN_EXPAND
    n_q_heads = d // ATTN_HEAD_DIM
    n_kv_heads = n_q_heads // ATTN_GROUP

    cond = jnp.zeros((MOD_ROWS, d), F32).at[:batch].set(c).at[batch].set(c_ctx)
    mod = _modulation(cond, w_mod, b_mod)
    mod = mod.reshape(depth, MOD_ROWS, N_MOD, d).transpose(0, 2, 1, 3).reshape(depth, N_MOD * MOD_ROWS, 1, d)

    narrow, wide = _even_layout(d)
    cos, sin = _rope_tables(seq, tm)

    xs = jnp.concatenate([x.reshape(batch * seq, d), ctx.reshape(batch * n_ctx, d)], axis=0)
    for layer in range(depth):
        last = layer == depth - 1
        j = layer // 2
        m = mod[layer]
        xs = _ffn(xs, rows.n_all, rows, m, (0, 1, 2), wg_all, wu_all, wd_all, (layer, 0), tm_mm)
        if layer % 2 == 0:
            w16, c16 = _gather_cols(mix_ab_w_in[j], narrow, MXU_N)
            w32, c32 = _gather_cols(mix_ab_w_in[j], wide, MXU_N)
            p16 = _proj(xs, rows, m, (3, 4), w16.astype(bf), tm_mm, _col_tile(w16.shape[1]), bf)
            p32 = _proj(xs, rows, m, (3, 4), w32.astype(bf), tm_mm, _col_tile(w32.shape[1]), F32)
            w2p = jnp.zeros((2, LANES, gla_w // 2), F32)
            for direction in range(2):
                r0 = direction * GLA_GATE_RANK
                w2p = w2p.at[direction, r0:r0 + GLA_GATE_RANK].set(gla_gate_w2[j, direction])
            ya_l, ya_c = _gla_mixer(p16, p32, rows, c16, c32, w2p.astype(bf), gla_gate_b[j].reshape(2, 1, -1),
                                    gla_norm_g[j].reshape(1, -1), gla_dk, gla_dv)
            yb_l, yb_c = _hgrn_mixer(p16, p32, rows, c16, c32, hgrn_lb_logits, hgrn_norm_g[j].reshape(1, -1), j,
                                     hgrn_heads, HGRN_EXPAND)
            lat_parts, ctx_parts = [ya_l, yb_l], [ya_c, yb_c]
            w_out = mix_ab_w_out[j].astype(bf)
        else:
            p = _proj(xs, rows, m, (3, 4), attn_w_in[j].astype(bf), tm_mm, _col_tile(attn_w_in.shape[2]))
            q, k, v = _qkv_prep(p, rows, cos, sin, attn_q_norm_g[j].reshape(1, -1), attn_k_norm_g[j].reshape(1, -1),
                                n_q_heads, n_kv_heads, tm)
            lat_parts = [_attention(q, k, v, rows, n_kv_heads, _row_tile(seq, seq, 256), False)]
            ctx_parts = None if last else [_attention(q, k, v, rows, n_kv_heads, n_ctx, True)]
            w_out = attn_w_out[j].astype(bf)
        if last:
            ctx_parts = None
        xs = _out_proj(xs, rows, m, 5, w_out, lat_parts, ctx_parts, tm)
        n_rows = rows.n_lat if last else rows.n_all
        xs = _ffn(xs, n_rows, rows, m, (6, 7, 8), wg_all, wu_all, wd_all, (layer, 1), tm_mm)
    out = _final_norm(xs, final_norm_g, tm)
    return out.reshape(batch, seq, d)
```

```python
import functools

import jax
import jax.numpy as jnp
import numpy as np
from jax import lax
from jax.experimental import pallas as pl
from jax.experimental.pallas import tpu as pltpu

F32 = jnp.float32
MXU_DTYPE = jnp.bfloat16

EPS = 1e-6
F_FLOOR = 1e-30
N_MOD = 9
GRID_W = 64
GLA_HEADS = 4
GLA_GATE_RANK = 16
GLA_GATE_TAU = 16.0
HGRN_EXPAND = 128
ATTN_HEAD_DIM = 128
ATTN_GROUP = 4
ATTN_KV_PER_STEP = 4
ROPE_THETA = 10000.0

LANES = 128
MXU_N = 256
VMEM_LIMIT_BYTES = 56 * 1024 * 1024

SCAN_CHUNK = 256
SUBLANES = 8
LOG2E = 1.4426950408889634
MOD_ROWS = 16


def _cparams(*sem):
    return pltpu.CompilerParams(dimension_semantics=sem, vmem_limit_bytes=VMEM_LIMIT_BYTES)


def _sigmoid(x):
    return 1.0 / (1.0 + jnp.exp(-x))


def _silu(x):
    return x * _sigmoid(x)


def _rms(x):
    return x * lax.rsqrt(jnp.mean(x * x, axis=-1, keepdims=True) + EPS)


def _dot(a, b):
    return jnp.dot(a, b, preferred_element_type=F32)


def _dot_nt(a, b):
    return lax.dot_general(a, b, (((1,), (1,)), ((), ())), preferred_element_type=F32)


def _dot_tn(a, b):
    return lax.dot_general(a, b, (((0,), (0,)), ((), ())), preferred_element_type=F32)


def _row_tile(n_lat, n_ctx_total, cap):
    t = cap
    while n_lat % t or n_ctx_total % t:
        t //= 2
    return t


def _col_tile(n, cap=1024):
    return max(t for t in range(MXU_N, cap + 1, MXU_N) if n % t == 0)


def _mod_kernel(c_ref, w_ref, b_ref, o_ref):
    s = _silu(c_ref[...]).astype(MXU_DTYPE)
    o_ref[...] = _dot(s, w_ref[...].astype(MXU_DTYPE)) + b_ref[...]


def _modulation(cond, w_mod, b_mod):
    depth, d, n = w_mod.shape
    tn = 1024
    return pl.pallas_call(
        _mod_kernel,
        grid=(depth, n // tn),
        in_specs=[pl.BlockSpec((MOD_ROWS, d), lambda l, j: (0, 0)),
                  pl.BlockSpec((None, d, tn), lambda l, j: (l, 0, j)),
                  pl.BlockSpec((None, 1, tn), lambda l, j: (l, 0, j))],
        out_specs=pl.BlockSpec((None, MOD_ROWS, tn), lambda l, j: (l, 0, j)),
        out_shape=jax.ShapeDtypeStruct((depth, MOD_ROWS, n), F32),
        compiler_params=_cparams("parallel", "parallel"),
        name="modulation",
    )(cond, w_mod, b_mod.reshape(depth, 1, n))


class _Rows:
    def __init__(self, batch, seq, ctx):
        self.batch, self.seq, self.ctx = batch, seq, ctx
        self.n_lat = batch * seq
        self.n_all = self.n_lat + batch * ctx

    def mod_spec(self, tm, width, slot, by_column=False):
        seq, batch = self.seq, self.batch

        def row(i):
            return slot * MOD_ROWS + jnp.minimum((i * tm) // seq, batch)
        if by_column:
            return pl.BlockSpec((None, 1, width), lambda i, j: (row(i), 0, j))
        return pl.BlockSpec((None, 1, width), lambda i, *_: (row(i), 0, 0))


def _ffn_up_kernel(x_ref, sh_ref, sc_ref, wg_ref, wu_ref, o_ref, h_scr):
    @pl.when(pl.program_id(1) == 0)
    def _():
        h = _rms(x_ref[...]) * (1.0 + sc_ref[...]) + sh_ref[...]
        h_scr[...] = h.astype(MXU_DTYPE)

    h = h_scr[...]
    o_ref[...] = (_silu(_dot(h, wg_ref[...])) * _dot(h, wu_ref[...])).astype(o_ref.dtype)


def _ffn_down_kernel(a_ref, wd_ref, x_ref, gt_ref, o_ref):
    o_ref[...] = x_ref[...] + 0.5 * gt_ref[...] * _dot(a_ref[...], wd_ref[...])


def _ffn(x, n_rows, rows, mod, slots, wg, wu, wd, which, tm):
    d = x.shape[1]
    dff = wg.shape[3]
    layer, half = which
    tf = _col_tile(dff, 512)
    act = pl.pallas_call(
        _ffn_up_kernel,
        grid=(n_rows // tm, dff // tf),
        in_specs=[pl.BlockSpec((tm, d), lambda i, j: (i, 0)),
                  rows.mod_spec(tm, d, slots[0]), rows.mod_spec(tm, d, slots[1]),
                  pl.BlockSpec((None, None, d, tf), lambda i, j: (layer, half, 0, j)),
                  pl.BlockSpec((None, None, d, tf), lambda i, j: (layer, half, 0, j))],
        out_specs=pl.BlockSpec((tm, tf), lambda i, j: (i, j)),
        out_shape=jax.ShapeDtypeStruct((n_rows, dff), MXU_DTYPE),
        scratch_shapes=[pltpu.VMEM((tm, d), MXU_DTYPE)],
        compiler_params=_cparams("parallel", "arbitrary"),
        name="ffn_up",
    )(x, mod, mod, wg, wu)
    tn = _col_tile(d, 512)
    return pl.pallas_call(
        _ffn_down_kernel,
        grid=(n_rows // tm, d // tn),
        in_specs=[pl.BlockSpec((tm, dff), lambda i, j: (i, 0)),
                  pl.BlockSpec((None, None, dff, tn), lambda i, j: (layer, half, 0, j)),
                  pl.BlockSpec((tm, tn), lambda i, j: (i, j)),
                  rows.mod_spec(tm, tn, slots[2], by_column=True)],
        out_specs=pl.BlockSpec((tm, tn), lambda i, j: (i, j)),
        out_shape=jax.ShapeDtypeStruct((n_rows, d), F32),
        compiler_params=_cparams("parallel", "arbitrary"),
        name="ffn_down",
    )(act, wd, x, mod)


def _proj_kernel(n_narrow, x_ref, sh_ref, sc_ref, w_ref, o16_ref, o32_ref, h_scr):
    j = pl.program_id(1)

    @pl.when(j == 0)
    def _():
        h = _rms(x_ref[...]) * (1.0 + sc_ref[...]) + sh_ref[...]
        h_scr[...] = h.astype(MXU_DTYPE)

    acc = _dot(h_scr[...], w_ref[...])

    @pl.when(j < n_narrow)
    def _():
        o16_ref[...] = acc.astype(o16_ref.dtype)

    @pl.when(j >= n_narrow)
    def _():
        o32_ref[...] = acc


def _proj(x, rows, mod, slots, w_narrow, w_wide, tm, tn):
    d = x.shape[1]
    n16, n32 = w_narrow.shape[1], w_wide.shape[1]
    assert n16 % tn == 0 and n32 % tn == 0
    n_narrow = n16 // tn
    return pl.pallas_call(
        functools.partial(_proj_kernel, n_narrow),
        grid=(rows.n_all // tm, (n16 + n32) // tn),
        in_specs=[pl.BlockSpec((tm, d), lambda i, j: (i, 0)),
                  rows.mod_spec(tm, d, slots[0]), rows.mod_spec(tm, d, slots[1]),
                  pl.BlockSpec((d, tn), lambda i, j: (0, j))],
        out_specs=[pl.BlockSpec((tm, tn), lambda i, j: (i, jnp.minimum(j, n_narrow - 1))),
                   pl.BlockSpec((tm, tn), lambda i, j: (i, jnp.maximum(j - n_narrow, 0)))],
        out_shape=[jax.ShapeDtypeStruct((rows.n_all, n16), MXU_DTYPE),
                   jax.ShapeDtypeStruct((rows.n_all, n32), F32)],
        scratch_shapes=[pltpu.VMEM((tm, d), MXU_DTYPE)],
        compiler_params=_cparams("parallel", "arbitrary"),
        name="mixer_in_proj",
    )(x, mod, mod, jnp.concatenate([w_narrow, w_wide], axis=1))


def _out_kernel(n_parts, widths, n_lat_tiles, has_ctx, *refs):
    x_ref, gt_ref, w_ref = refs[0], refs[1], refs[2]
    lat = refs[3:3 + n_parts]
    ctx = refs[3 + n_parts:3 + 2 * n_parts] if has_ctx else ()
    o_ref = refs[-1]

    def run(parts):
        acc = None
        off = 0
        for p, wd in zip(parts, widths):
            t = _dot(p[...], w_ref[off:off + wd, :])
            acc = t if acc is None else acc + t
            off += wd
        o_ref[...] = x_ref[...] + gt_ref[...] * acc

    if has_ctx:
        i = pl.program_id(0)
        pl.when(i < n_lat_tiles)(lambda: run(lat))
        pl.when(i >= n_lat_tiles)(lambda: run(ctx))
    else:
        run(lat)


def _out_proj(x, rows, mod, slot, w, lat_parts, ctx_parts, tm):
    d = x.shape[1]
    has_ctx = ctx_parts is not None
    n_rows = rows.n_all if has_ctx else rows.n_lat
    n_lat_tiles = rows.n_lat // tm
    widths = tuple(p.shape[1] for p in lat_parts)
    specs = [pl.BlockSpec((tm, d), lambda i: (i, 0)), rows.mod_spec(tm, d, slot),
             pl.BlockSpec(w.shape, lambda i: (0, 0))]
    specs += [pl.BlockSpec((tm, wd), lambda i: (jnp.minimum(i, n_lat_tiles - 1), 0)) for wd in widths]
    args = [x, mod, w] + list(lat_parts)
    if has_ctx:
        specs += [pl.BlockSpec((tm, wd), lambda i: (jnp.maximum(i - n_lat_tiles, 0), 0)) for wd in widths]
        args += list(ctx_parts)
    return pl.pallas_call(
        functools.partial(_out_kernel, len(widths), widths, n_lat_tiles, has_ctx),
        grid=(n_rows // tm,),
        in_specs=specs,
        out_specs=pl.BlockSpec((tm, d), lambda i: (i, 0)),
        out_shape=jax.ShapeDtypeStruct((n_rows, d), F32),
        compiler_params=_cparams("parallel"),
        name="mixer_out_proj",
    )(*args)


def _scan_tables(chunk):
    c = chunk
    t = np.arange(c)[:, None]
    u = np.arange(c)[None, :]
    tri = (u <= t).astype(np.float32)
    x = t ^ u
    lvl = np.zeros((c, c), np.int32)
    nz = x > 0
    lvl[nz] = int(np.log2(c)) - np.floor(np.log2(x[nz])).astype(np.int32)
    return tri, lvl[:c // 2, :c // 2]


def _level_ref(cum, m):
    c, dk = cum.shape
    if 2 * m >= SUBLANES:
        x = cum.reshape(c // (2 * m), 2 * m, dk)
        return jnp.broadcast_to(x[:, m - 1:m, :], x.shape).reshape(c, dk)
    x = cum.reshape(c // SUBLANES, SUBLANES, dk)
    sub = lax.broadcasted_iota(jnp.int32, x.shape, 1)
    lo, hi = (jnp.broadcast_to(x[:, r:r + 1, :], x.shape) for r in (m - 1, 2 * m + m - 1))
    return jnp.where(sub < 2 * m, lo, hi).reshape(c, dk)


def _chunk_local(q, ks, vb, gs, tri, lvl):
    c, dk = q.shape
    n_levels = int(np.log2(c))
    g_f, g_b = (g * LOG2E for g in gs)
    pieces = []
    for g in (g_f, g_b):
        hi = g.astype(MXU_DTYPE)
        pieces += [hi, (g - hi.astype(F32)).astype(MXU_DTYPE)]
    ex = _dot(tri, jnp.concatenate(pieces, axis=1))
    cum_f = ex[:, :dk] + ex[:, dk:2 * dk]
    cum_b = ex[:, 2 * dk:3 * dk] + ex[:, 3 * dk:]
    tot_f, tot_b = cum_f[c - 1:c], cum_b[c - 1:c]
    a_b = cum_b - g_b
    same_k = ks[0] is ks[1]
    k_sum = ks[0] * 2.0 if same_k else ks[0] + ks[1]
    h = c // 2

    def diag_blocks(zq, zk):
        zq, zk = zq.astype(MXU_DTYPE), zk.astype(MXU_DTYPE)
        return [_dot_nt(zq[:h], zk[:h]), _dot_nt(zq[h:], zk[h:])]

    attn = [jnp.where(lvl == 0, a, 0.0) for a in diag_blocks(q, k_sum)]
    row = lax.broadcasted_iota(jnp.int32, (c, 1), 0)
    for li in range(n_levels):
        m = c >> (li + 1)
        second = (row & m) != 0
        k_m = ks[0] if same_k else jnp.where(second, ks[1], ks[0])
        if m == 1:
            zq = q * jnp.exp2(jnp.where(second, g_f, g_b))
            zk = k_m
        else:
            d_f = cum_f - _level_ref(cum_f, m)
            d_b = a_b - _level_ref(cum_b, m)
            zq = q * jnp.exp2(jnp.where(second, d_f, -d_b))
            zk = k_m * jnp.exp2(jnp.where(second, d_b, -d_f))
        if m == h:
            zq, zk = zq.astype(MXU_DTYPE), zk.astype(MXU_DTYPE)
            lower, upper = _dot_nt(zq[h:], zk[:h]), _dot_nt(zq[:h], zk[h:])
        else:
            attn = [jnp.where(lvl == li + 1, a, old) for a, old in zip(diag_blocks(zq, zk), attn)]
    attn = jnp.concatenate([jnp.concatenate([attn[0], upper], axis=1),
                            jnp.concatenate([lower, attn[1]], axis=1)], axis=0)
    o = _dot(attn.astype(MXU_DTYPE), vb)
    fwd = ((q * jnp.exp2(cum_f)).astype(MXU_DTYPE), (ks[0] * jnp.exp2(tot_f - cum_f)).astype(MXU_DTYPE),
           jnp.exp2(tot_f))
    bwd = ((q * jnp.exp2(tot_b - a_b)).astype(MXU_DTYPE), (ks[1] * jnp.exp2(a_b)).astype(MXU_DTYPE),
           jnp.exp2(tot_b))
    return o, fwd, bwd


def _bidir_scan(segments, chunk, gates, finish, tabs, scr):
    tri, lvl = tabs
    acc, vt_s, qh_s, kh_s, dec_s, st_s = scr
    n_seg = [n_rows // chunk for n_rows, _ in segments]
    n_all = sum(n_seg)

    def rows_of(j):
        return pl.ds(pl.multiple_of(j * chunk, chunk), chunk)

    base = 0
    for (n_rows, seg), n in zip(segments, n_seg):
        def local(ci, _, seg=seg, base=base):
            q, ks, v, gs = gates(seg, rows_of(ci))
            j = base + ci
            vb = v.astype(MXU_DTYPE)
            vt_s[j] = vb.T
            o, fwd, bwd = _chunk_local(q, ks, vb, gs, tri, lvl)
            for d, (qh, kh, dec) in enumerate((fwd, bwd)):
                qh_s[d, rows_of(j), :] = qh
                kh_s[d, rows_of(j), :] = kh
                dec_s[d, pl.ds(pl.multiple_of(j * SUBLANES, SUBLANES), SUBLANES), :] = jnp.broadcast_to(
                    dec, (SUBLANES, dec.shape[1]))
            acc[rows_of(j), :] = o
            return 0
        lax.fori_loop(0, n, local, 0)
        base += n

    st_s[...] = jnp.zeros_like(st_s)
    n_first = n_seg[0]

    def carry(i, _):
        j_b = jnp.where(i < n_first, n_first - 1 - i, n_all + n_first - 1 - i)
        for d, j in enumerate((i, j_b)):
            st = st_s[d]
            acc[rows_of(j), :] += _dot_nt(qh_s[d, rows_of(j), :], st.astype(MXU_DTYPE))
            dec = dec_s[d, pl.ds(pl.multiple_of(j * SUBLANES, SUBLANES), 1), :]
            st_s[d] = st * dec + _dot(vt_s[j], kh_s[d, rows_of(j), :])
        return 0
    lax.fori_loop(0, n_all, carry, 0, unroll=True)

    base = 0
    for (n_rows, seg), n in zip(segments, n_seg):
        def done(ci, _, seg=seg, base=base):
            finish(seg, rows_of(ci), acc[rows_of(base + ci), :])
            return 0
        lax.fori_loop(0, n, done, 0)
        base += n


def _log_sigmoid(x):
    return jnp.minimum(x, 0.0) - jnp.log(1.0 + jnp.exp(-jnp.abs(x)))


def _gla_kernel(chunk, seq, ctx, scale,
                tri, lvl, w2_ref, b_ref, gain_ref,
                q_c, k_c, v_c, gr_c, lr_c, q_l, k_l, v_l, gr_l, lr_l,
                y_c, y_l, *scr):
    ins = ((q_c, k_c, v_c, lr_c, gr_c, y_c), (q_l, k_l, v_l, lr_l, gr_l, y_l))

    def gates(seg, sl):
        q_r, k_r, v_r, lr_r = ins[seg][:4]
        lr = lr_r[sl, :].astype(MXU_DTYPE)
        gs = tuple(_log_sigmoid(_dot(lr, w2_ref[d]) + b_ref[d]) * (1.0 / GLA_GATE_TAU) for d in range(2))
        k = k_r[sl, :].astype(F32)
        return q_r[sl, :].astype(F32) * scale, (k, k), v_r[sl, :], gs

    def finish(seg, sl, o):
        gr_r, y_r = ins[seg][4:]
        y_r[sl, :] = (_rms(o) * gain_ref[...] * _silu(gr_r[sl, :].astype(F32))).astype(y_r.dtype)

    _bidir_scan([(ctx, 0), (seq, 1)], chunk, gates, finish, (tri[...], lvl[...]), scr)


def _hgrn_kernel(chunk, seq, ctx, layer_j,
                 tri, lvl, lbl_ref, gain_ref,
                 q_c, i_c, hg_c, ff_c, fb_c, q_l, i_l, hg_l, ff_l, fb_l,
                 y_c, y_l, *scr):
    ins = ((q_c, (ff_c, fb_c), i_c, hg_c, y_c), (q_l, (ff_l, fb_l), i_l, hg_l, y_l))
    lbs = []
    for direction in range(2):
        logits = lbl_ref[direction]
        pe = jnp.exp(logits - jnp.max(logits, axis=0, keepdims=True))
        p = pe / jnp.sum(pe, axis=0, keepdims=True)
        lbs.append(jnp.sum(p[0:layer_j + 1], axis=0, keepdims=True) - p[0:1])

    def gates(seg, sl):
        q_r, f_rs, i_r = ins[seg][:3]
        ks, gs = [], []
        for d in range(2):
            lb = lbs[d]
            x = f_rs[d][sl, :]
            en = jnp.exp(-jnp.abs(x))
            inv = 1.0 / (1.0 + en)
            pos = x >= 0.0
            sig_p = jnp.where(pos, inv, en * inv)
            sig_n = jnp.where(pos, en * inv, inv)
            gs.append(jnp.log(jnp.maximum(lb + (1.0 - lb) * sig_p, F_FLOOR)))
            ks.append((1.0 - lb) * sig_n)
        return _silu(q_r[sl, :].astype(F32)), ks, i_r[sl, :], gs

    def finish(seg, sl, o):
        hg_r, y_r = ins[seg][3:]
        y_r[sl, :] = (_rms(o) * gain_ref[...] * _silu(hg_r[sl, :].astype(F32))).astype(y_r.dtype)

    _bidir_scan([(ctx, 0), (seq, 1)], chunk, gates, finish, (tri[...], lvl[...]), scr)


def _scan_scratch(chunk, n_rows, dk, dv):
    n_chunks = n_rows // chunk
    return [pltpu.VMEM((n_rows, dv), F32), pltpu.VMEM((n_chunks, dv, chunk), MXU_DTYPE),
            pltpu.VMEM((2, n_rows, dk), MXU_DTYPE), pltpu.VMEM((2, n_rows, dk), MXU_DTYPE),
            pltpu.VMEM((2, n_chunks * SUBLANES, dk), F32), pltpu.VMEM((2, dv, dk), F32)]


def _scan_chunk(rows):
    return _row_tile(rows.seq, rows.ctx, SCAN_CHUNK)


def _scan_table_args(chunk):
    tri, lvl = _scan_tables(chunk)
    args = [jnp.asarray(tri, MXU_DTYPE), jnp.asarray(lvl)]
    specs = [pl.BlockSpec(a.shape, lambda b, h: (0, 0)) for a in args]
    return args, specs


def _seg_inputs(rows, cols):
    ctx_base = rows.n_lat // rows.ctx
    specs, arrays = [], []
    for n_rows, base in ((rows.ctx, ctx_base), (rows.seq, 0)):
        for arr, w, c0, per_head in cols:
            specs.append(pl.BlockSpec(
                (n_rows, w), lambda b, h, base=base, cb=c0 // w, ph=int(per_head): (base + b, cb + ph * h)))
            arrays.append(arr)
    return specs, arrays


def _scan_call(body, name, rows, n_heads, dk, dv, chunk, param_specs, params, seg_cols):
    targs, tspecs = _scan_table_args(chunk)
    seg_specs, seg_arrays = _seg_inputs(rows, seg_cols)
    width = n_heads * dv
    y_c, y_l = pl.pallas_call(
        body,
        grid=(rows.batch, n_heads),
        in_specs=tspecs + param_specs + seg_specs,
        out_specs=[pl.BlockSpec((rows.ctx, dv), lambda b, h: (b, h)),
                   pl.BlockSpec((rows.seq, dv), lambda b, h: (b, h))],
        out_shape=[jax.ShapeDtypeStruct((rows.batch * rows.ctx, width), MXU_DTYPE),
                   jax.ShapeDtypeStruct((rows.n_lat, width), MXU_DTYPE)],
        scratch_shapes=_scan_scratch(chunk, rows.ctx + rows.seq, dk, dv),
        compiler_params=_cparams("parallel", "parallel"),
        name=name,
    )(*targs, *params, *seg_arrays)
    return y_l, y_c


def _gla_mixer(p16, p32, rows, c16, c32, w2p, b2, gain, dk, dv):
    chunk = _scan_chunk(rows)
    u = LANES
    param_specs = [pl.BlockSpec((2, u, dk), lambda b, h: (0, 0, h)),
                   pl.BlockSpec((2, 1, dk), lambda b, h: (0, 0, h)),
                   pl.BlockSpec((1, dv), lambda b, h: (0, 0))]
    seg_cols = [(p16, dk, c16["gq"], True), (p16, dk, c16["gk"], True), (p16, dv, c16["gv"], True),
                (p16, dv, c16["gr"], True), (p32, u, c32["glr"], False)]
    return _scan_call(functools.partial(_gla_kernel, chunk, rows.seq, rows.ctx, dk ** -0.5), "gla_scan",
                      rows, GLA_HEADS, dk, dv, chunk, param_specs, [w2p, b2, gain], seg_cols)


def _hgrn_mixer(p16, p32, rows, c16, c32, lb_logits, gain, layer_j, n_heads, dk):
    chunk = _scan_chunk(rows)
    n_even = lb_logits.shape[1]
    param_specs = [pl.BlockSpec((2, n_even, dk), lambda b, h: (0, 0, h)),
                   pl.BlockSpec((1, dk), lambda b, h: (0, 0))]
    seg_cols = [(p16, dk, c16["hq"], True), (p16, dk, c16["hi"], True), (p16, dk, c16["hg"], True),
                (p32, dk, c32["hf"], True), (p32, dk, c32["hf"] + n_heads * dk, True)]
    return _scan_call(functools.partial(_hgrn_kernel, chunk, rows.seq, rows.ctx, layer_j), "hgrn_scan",
                      rows, n_heads, dk, dk, chunk, param_specs, [lb_logits, gain], seg_cols)


def _qkv_kernel(n_q_tiles, n_kv, x_ref, sh_ref, sc_ref, w_ref, cos_ref, sin_ref, qg_ref, kg_ref,
                q_ref, k_ref, v_ref, h_scr):
    j = pl.program_id(1)

    @pl.when(j == 0)
    def _():
        h = _rms(x_ref[...]) * (1.0 + sc_ref[...]) + sh_ref[...]
        h_scr[...] = h.astype(MXU_DTYPE)

    hd = ATTN_HEAD_DIM
    cos, sin = cos_ref[...], sin_ref[...]
    h_in = h_scr[...]
    step = MXU_N

    def heads(c0):
        acc = _dot(h_in, w_ref[:, c0:c0 + step])
        return [acc[:, i * hd:(i + 1) * hd] for i in range(step // hd)]

    def rope(a):
        return a * cos + pltpu.roll(a, hd // 2, axis=1) * sin

    def normed_heads(n_cols, out_ref, gain):
        for c0 in range(0, n_cols, step):
            for i, a in enumerate(heads(c0)):
                out_ref[:, c0 + i * hd:c0 + (i + 1) * hd] = rope(_rms(a) * gain).astype(out_ref.dtype)

    @pl.when(j < n_q_tiles)
    def _():
        normed_heads(w_ref.shape[1], q_ref, qg_ref[...] * (hd ** -0.5 * LOG2E))

    @pl.when(j == n_q_tiles)
    def _():
        for c0 in range(n_kv * hd, w_ref.shape[1], step):
            v_ref[:, c0 - n_kv * hd:c0 - n_kv * hd + step] = _dot(h_in, w_ref[:, c0:c0 + step]).astype(v_ref.dtype)
        normed_heads(n_kv * hd, k_ref, kg_ref[...])


def _qkv_proj(x, rows, mod, slots, w, cos, sin, qg, kg, n_q, n_kv, tm):
    d = x.shape[1]
    hd = ATTN_HEAD_DIM
    tn = 2 * n_kv * hd
    assert (n_q * hd) % tn == 0 and w.shape[1] == n_q * hd + tn
    n_q_tiles = n_q * hd // tn
    n_lat_tiles = rows.n_lat // tm
    per_seq = rows.seq // tm
    tab = pl.BlockSpec((tm, hd), lambda i, j: (jnp.where(i < n_lat_tiles, i % per_seq, per_seq), 0))
    return pl.pallas_call(
        functools.partial(_qkv_kernel, n_q_tiles, n_kv),
        grid=(rows.n_all // tm, n_q_tiles + 1),
        in_specs=[pl.BlockSpec((tm, d), lambda i, j: (i, 0)),
                  rows.mod_spec(tm, d, slots[0]), rows.mod_spec(tm, d, slots[1]),
                  pl.BlockSpec((d, tn), lambda i, j: (0, j)), tab, tab,
                  pl.BlockSpec((1, hd), lambda i, j: (0, 0)), pl.BlockSpec((1, hd), lambda i, j: (0, 0))],
        out_specs=[pl.BlockSpec((tm, tn), lambda i, j: (i, jnp.minimum(j, n_q_tiles - 1))),
                   pl.BlockSpec((tm, n_kv * hd), lambda i, j: (i, 0)),
                   pl.BlockSpec((tm, n_kv * hd), lambda i, j: (i, 0))],
        out_shape=[jax.ShapeDtypeStruct((rows.n_all, n_q * hd), MXU_DTYPE),
                   jax.ShapeDtypeStruct((rows.n_all, n_kv * hd), MXU_DTYPE),
                   jax.ShapeDtypeStruct((rows.n_all, n_kv * hd), MXU_DTYPE)],
        scratch_shapes=[pltpu.VMEM((tm, d), MXU_DTYPE)],
        compiler_params=_cparams("parallel", "arbitrary"),
        name="qkv_proj_norm_rope",
    )(x, mod, mod, w, cos, sin, qg, kg)


def _attn_kernel(n_kv_sets, kv_heads, q_ref, *refs):
    kv = refs[:2 * n_kv_sets]
    o_ref, s_scr = refs[2 * n_kv_sets:]
    hd = ATTN_HEAD_DIM
    n_heads = kv_heads * ATTN_GROUP
    bounds = np.cumsum([0] + [kv[2 * i].shape[0] for i in range(n_kv_sets)])

    def scores(g):
        q = q_ref[:, g * hd:(g + 1) * hd]
        c0 = (g // ATTN_GROUP) * hd
        for i in range(n_kv_sets):
            s_scr[g % 2, :, bounds[i]:bounds[i + 1]] = _dot_nt(q, kv[2 * i][:, c0:c0 + hd])

    scores(0)
    for g in range(n_heads):
        if g + 1 < n_heads:
            scores(g + 1)
        s = s_scr[g % 2]
        p = jnp.exp2(s - jnp.max(s, axis=-1, keepdims=True))
        den = jnp.sum(p, axis=-1, keepdims=True)
        pb = p.astype(MXU_DTYPE)
        c0 = (g // ATTN_GROUP) * hd
        acc = functools.reduce(jnp.add, [_dot(pb[:, bounds[i]:bounds[i + 1]], kv[2 * i + 1][:, c0:c0 + hd])
                                         for i in range(n_kv_sets)])
        o_ref[:, g * hd:(g + 1) * hd] = (acc / den).astype(o_ref.dtype)


def _attention(q, k, v, rows, n_kv, tq, ctx_queries):
    hd = ATTN_HEAD_DIM
    kvh = ATTN_KV_PER_STEP if n_kv % ATTN_KV_PER_STEP == 0 else 1
    kw = kvh * hd
    gw = ATTN_GROUP * kw
    ctx_base = rows.n_lat // rows.ctx
    ctx_kv = pl.BlockSpec((rows.ctx, kw), lambda b, h, i: (ctx_base + b, h))
    if ctx_queries:
        n_q, per = rows.batch * rows.ctx, 1
        tq = rows.ctx
        q_spec = pl.BlockSpec((tq, gw), lambda b, h, i: (ctx_base + b, h))
        kv_specs, kv_args, n_keys = [ctx_kv, ctx_kv], [k, v], rows.ctx
    else:
        n_q, per = rows.n_lat, rows.seq // tq
        q_spec = pl.BlockSpec((tq, gw), lambda b, h, i: (b * per + i, h))
        lat_kv = pl.BlockSpec((rows.seq, kw), lambda b, h, i: (b, h))
        kv_specs, kv_args, n_keys = [lat_kv, lat_kv, ctx_kv, ctx_kv], [k, v, k, v], rows.seq + rows.ctx
    return pl.pallas_call(
        functools.partial(_attn_kernel, len(kv_args) // 2, kvh),
        grid=(rows.batch, n_kv // kvh, per),
        in_specs=[q_spec] + kv_specs,
        out_specs=pl.BlockSpec((tq, gw), lambda b, h, i: (b * per + i, h)),
        out_shape=jax.ShapeDtypeStruct((n_q, n_kv * ATTN_GROUP * hd), MXU_DTYPE),
        scratch_shapes=[pltpu.VMEM((2, tq, n_keys), F32)],
        compiler_params=_cparams("parallel", "parallel", "arbitrary"),
        name="gqa_ctx" if ctx_queries else "gqa_latent",
    )(q, *kv_args)


def _rope_perm():
    q = ATTN_HEAD_DIM // 4
    return np.concatenate([np.arange(0, q), np.arange(2 * q, 3 * q), np.arange(q, 2 * q), np.arange(3 * q, 4 * q)])


def _permute_heads(a, n_heads):
    lead = a.shape[:-1]
    return a.reshape(*lead, n_heads, ATTN_HEAD_DIM)[..., _rope_perm()].reshape(*lead, n_heads * ATTN_HEAD_DIM)


def _rope_tables(seq, pad_rows):
    hd = ATTN_HEAD_DIM
    half = hd // 2
    pos = jnp.arange(seq)
    inv_freq = ROPE_THETA ** (-jnp.arange(0, half, 2, dtype=F32) / half)
    ang = jnp.stack([pos // GRID_W, pos % GRID_W], axis=-1).astype(F32)[:, :, None] * inv_freq
    ang = ang.reshape(seq, half)
    cos = jnp.concatenate([jnp.cos(ang), jnp.cos(ang)], axis=-1)
    sin = jnp.concatenate([-jnp.sin(ang), jnp.sin(ang)], axis=-1)
    cos = jnp.concatenate([cos, jnp.ones((pad_rows, hd), F32)], axis=0)
    sin = jnp.concatenate([sin, jnp.zeros((pad_rows, hd), F32)], axis=0)
    return cos, sin


def _final_kernel(x_ref, g_ref, o_ref):
    o_ref[...] = _rms(x_ref[...]) * g_ref[...]


def _final_norm(x, gain, tm):
    n, d = x.shape
    return pl.pallas_call(
        _final_kernel,
        grid=(n // tm,),
        in_specs=[pl.BlockSpec((tm, d), lambda i: (i, 0)), pl.BlockSpec((1, d), lambda i: (0, 0))],
        out_specs=pl.BlockSpec((tm, d), lambda i: (i, 0)),
        out_shape=jax.ShapeDtypeStruct((n, d), F32),
        compiler_params=_cparams("parallel"),
        name="final_norm",
    )(x, gain.reshape(1, d))


def _even_layout(d):
    gla_w = d // 2
    gla_qk = gla_w // 2
    hgrn_w = d // 2
    names = ["gq", "gk", "gv", "gr", "glr", "hq", "hf", "hi", "hg"]
    widths = [gla_qk, gla_qk, gla_w, gla_w, 2 * GLA_GATE_RANK, hgrn_w, 2 * hgrn_w, hgrn_w, hgrn_w]
    src = dict(zip(names, zip((int(o) for o in np.cumsum([0] + widths[:-1])), widths)))
    narrow = {k: src[k] for k in ("gq", "gk", "gv", "gr", "hq", "hi", "hg")}
    wide = {k: src[k] for k in ("glr", "hf")}
    return narrow, wide


def _gather_cols(w, group, pad_to):
    parts, offs, off = [], {}, 0
    for name, (c0, width) in group.items():
        padded = -(-width // pad_to) * pad_to
        parts.append(w[:, c0:c0 + width])
        if padded > width:
            parts.append(jnp.zeros((w.shape[0], padded - width), w.dtype))
        offs[name] = off
        off += padded
    return jnp.concatenate(parts, axis=1), offs


def kernel(x, c, ctx, c_ctx, w_mod, b_mod, ffn_w_gate, ffn_w_up, ffn_w_down, mix_ab_w_in, gla_gate_w2, gla_gate_b,
           gla_norm_g, hgrn_lb_logits, hgrn_norm_g, mix_ab_w_out, attn_w_in, attn_q_norm_g, attn_k_norm_g,
           attn_w_out, final_norm_g):
    batch, seq, d = x.shape
    n_ctx = ctx.shape[1]
    depth = w_mod.shape[0]
    rows = _Rows(batch, seq, n_ctx)
    tm = _row_tile(seq, batch * n_ctx, 512)
    tm_mm = _row_tile(seq, batch * n_ctx, 1024)
    bf = MXU_DTYPE
    wg_all, wu_all, wd_all = ffn_w_gate.astype(bf), ffn_w_up.astype(bf), ffn_w_down.astype(bf)

    gla_w = d // 2
    gla_dv = gla_w // GLA_HEADS
    gla_dk = gla_dv // 2
    hgrn_w = d // 2
    hgrn_heads = hgrn_w // HGRN_EXPAND
    n_q_heads = d // ATTN_HEAD_DIM
    n_kv_heads = n_q_heads // ATTN_GROUP

    cond = jnp.zeros((MOD_ROWS, d), F32).at[:batch].set(c).at[batch].set(c_ctx)
    mod = _modulation(cond, w_mod, b_mod)
    mod = mod.reshape(depth, MOD_ROWS, N_MOD, d).transpose(0, 2, 1, 3).reshape(depth, N_MOD * MOD_ROWS, 1, d)

    narrow, wide = _even_layout(d)
    cos, sin = _rope_tables(seq, tm_mm)

    xs = jnp.concatenate([x.reshape(batch * seq, d), ctx.reshape(batch * n_ctx, d)], axis=0)
    for layer in range(depth):
        last = layer == depth - 1
        j = layer // 2
        m = mod[layer]
        xs = _ffn(xs, rows.n_all, rows, m, (0, 1, 2), wg_all, wu_all, wd_all, (layer, 0), tm_mm)
        if layer % 2 == 0:
            w16, c16 = _gather_cols(mix_ab_w_in[j], narrow, MXU_N)
            w32, c32 = _gather_cols(mix_ab_w_in[j], wide, MXU_N)
            tn = _col_tile(np.gcd(w16.shape[1], w32.shape[1]))
            p16, p32 = _proj(xs, rows, m, (3, 4), w16.astype(bf), w32.astype(bf), tm_mm, tn)
            w2p = jnp.zeros((2, LANES, gla_w // 2), F32)
            for direction in range(2):
                r0 = direction * GLA_GATE_RANK
                w2p = w2p.at[direction, r0:r0 + GLA_GATE_RANK].set(gla_gate_w2[j, direction])
            ya_l, ya_c = _gla_mixer(p16, p32, rows, c16, c32, w2p.astype(bf), gla_gate_b[j].reshape(2, 1, -1),
                                    gla_norm_g[j].reshape(1, -1), gla_dk, gla_dv)
            yb_l, yb_c = _hgrn_mixer(p16, p32, rows, c16, c32, hgrn_lb_logits, hgrn_norm_g[j].reshape(1, -1), j,
                                     hgrn_heads, HGRN_EXPAND)
            lat_parts, ctx_parts = [ya_l, yb_l], [ya_c, yb_c]
            w_out = mix_ab_w_out[j].astype(bf)
        else:
            n_qk = (n_q_heads + n_kv_heads) * ATTN_HEAD_DIM
            w_qkv = jnp.concatenate([_permute_heads(attn_w_in[j][:, :n_qk], n_q_heads + n_kv_heads),
                                     attn_w_in[j][:, n_qk:]], axis=1).astype(bf)
            q, k, v = _qkv_proj(xs, rows, m, (3, 4), w_qkv, cos, sin,
                                _permute_heads(attn_q_norm_g[j].reshape(1, -1), 1),
                                _permute_heads(attn_k_norm_g[j].reshape(1, -1), 1),
                                n_q_heads, n_kv_heads, tm_mm)
            lat_parts = [_attention(q, k, v, rows, n_kv_heads, _row_tile(seq, seq, 256), False)]
            ctx_parts = None if last else [_attention(q, k, v, rows, n_kv_heads, n_ctx, True)]
            w_out = attn_w_out[j].astype(bf)
        if last:
            ctx_parts = None
        xs = _out_proj(xs, rows, m, 5, w_out, lat_parts, ctx_parts, tm)
        n_rows = rows.n_lat if last else rows.n_all
        xs = _ffn(xs, n_rows, rows, m, (6, 7, 8), wg_all, wu_all, wd_all, (layer, 1), tm_mm)
    out = _final_norm(xs, final_norm_g, tm)
    return out.reshape(batch, seq, d)
```

```python
import functools

import jax
import jax.numpy as jnp
import numpy as np
from jax import lax
from jax.experimental import pallas as pl
from jax.experimental.pallas import tpu as pltpu

F32 = jnp.float32
MXU_DTYPE = jnp.bfloat16

EPS = 1e-6
F_FLOOR = 1e-30
N_MOD = 9
GRID_W = 64
GLA_HEADS = 4
GLA_GATE_RANK = 16
GLA_GATE_TAU = 16.0
HGRN_EXPAND = 128
ATTN_HEAD_DIM = 128
ATTN_GROUP = 4
ATTN_KV_PER_STEP = 4
ROPE_THETA = 10000.0

LANES = 128
MXU_N = 256
VMEM_LIMIT_BYTES = 56 * 1024 * 1024

SCAN_CHUNK = 256
SUBLANES = 8
LOG2E = 1.4426950408889634
MOD_ROWS = 16


def _cparams(*sem):
    return pltpu.CompilerParams(dimension_semantics=sem, vmem_limit_bytes=VMEM_LIMIT_BYTES)


def _sigmoid(x):
    return 1.0 / (1.0 + jnp.exp(-x))


def _silu(x):
    return x * _sigmoid(x)


def _rms(x):
    return x * lax.rsqrt(jnp.mean(x * x, axis=-1, keepdims=True) + EPS)


def _dot(a, b):
    return jnp.dot(a, b, preferred_element_type=F32)


def _dot_nt(a, b):
    return lax.dot_general(a, b, (((1,), (1,)), ((), ())), preferred_element_type=F32)


def _dot_tn(a, b):
    return lax.dot_general(a, b, (((0,), (0,)), ((), ())), preferred_element_type=F32)


def _row_tile(n_lat, n_ctx_total, cap):
    t = cap
    while n_lat % t or n_ctx_total % t:
        t //= 2
    return t


def _col_tile(n, cap=1024):
    return max(t for t in range(MXU_N, cap + 1, MXU_N) if n % t == 0)


def _mod_kernel(c_ref, w_ref, b_ref, o_ref):
    s = _silu(c_ref[...]).astype(MXU_DTYPE)
    o_ref[...] = _dot(s, w_ref[...].astype(MXU_DTYPE)) + b_ref[...]


def _modulation(cond, w_mod, b_mod):
    depth, d, n = w_mod.shape
    tn = 1024
    return pl.pallas_call(
        _mod_kernel,
        grid=(depth, n // tn),
        in_specs=[pl.BlockSpec((MOD_ROWS, d), lambda l, j: (0, 0)),
                  pl.BlockSpec((None, d, tn), lambda l, j: (l, 0, j)),
                  pl.BlockSpec((None, 1, tn), lambda l, j: (l, 0, j))],
        out_specs=pl.BlockSpec((None, MOD_ROWS, tn), lambda l, j: (l, 0, j)),
        out_shape=jax.ShapeDtypeStruct((depth, MOD_ROWS, n), F32),
        compiler_params=_cparams("parallel", "parallel"),
        name="modulation",
    )(cond, w_mod, b_mod.reshape(depth, 1, n))


class _Rows:
    def __init__(self, batch, seq, ctx):
        self.batch, self.seq, self.ctx = batch, seq, ctx
        self.n_lat = batch * seq
        self.n_all = self.n_lat + batch * ctx

    def mod_spec(self, tm, width, slot, by_column=False):
        seq, batch = self.seq, self.batch

        def row(i):
            return slot * MOD_ROWS + jnp.minimum((i * tm) // seq, batch)
        if by_column:
            return pl.BlockSpec((None, 1, width), lambda i, j: (row(i), 0, j))
        return pl.BlockSpec((None, 1, width), lambda i, *_: (row(i), 0, 0))


def _ffn_up_kernel(x_ref, sh_ref, sc_ref, wg_ref, wu_ref, o_ref, h_scr):
    @pl.when(pl.program_id(1) == 0)
    def _():
        h = _rms(x_ref[...]) * (1.0 + sc_ref[...]) + sh_ref[...]
        h_scr[...] = h.astype(MXU_DTYPE)

    h = h_scr[...]
    a = _dot(h, wg_ref[...].astype(MXU_DTYPE))
    u = _dot(h, wu_ref[...].astype(MXU_DTYPE))
    o_ref[...] = (_silu(a) * u).astype(o_ref.dtype)


def _ffn_down_kernel(a_ref, wd_ref, x_ref, gt_ref, o_ref):
    o_ref[...] = x_ref[...] + 0.5 * gt_ref[...] * _dot(a_ref[...], wd_ref[...])


def _ffn(x, n_rows, rows, mod, slots, wg, wu, wd, which, tm):
    d = x.shape[1]
    dff = wg.shape[3]
    layer, half = which
    tf = _col_tile(dff, 512)
    act = pl.pallas_call(
        _ffn_up_kernel,
        grid=(n_rows // tm, dff // tf),
        in_specs=[pl.BlockSpec((tm, d), lambda i, j: (i, 0)),
                  rows.mod_spec(tm, d, slots[0]), rows.mod_spec(tm, d, slots[1]),
                  pl.BlockSpec((None, None, d, tf), lambda i, j: (layer, half, 0, j)),
                  pl.BlockSpec((None, None, d, tf), lambda i, j: (layer, half, 0, j))],
        out_specs=pl.BlockSpec((tm, tf), lambda i, j: (i, j)),
        out_shape=jax.ShapeDtypeStruct((n_rows, dff), MXU_DTYPE),
        scratch_shapes=[pltpu.VMEM((tm, d), MXU_DTYPE)],
        compiler_params=_cparams("parallel", "arbitrary"),
        name="ffn_up",
    )(x, mod, mod, wg, wu)
    tn = _col_tile(d, 512)
    return pl.pallas_call(
        _ffn_down_kernel,
        grid=(n_rows // tm, d // tn),
        in_specs=[pl.BlockSpec((tm, dff), lambda i, j: (i, 0)),
                  pl.BlockSpec((None, None, dff, tn), lambda i, j: (layer, half, 0, j)),
                  pl.BlockSpec((tm, tn), lambda i, j: (i, j)),
                  rows.mod_spec(tm, tn, slots[2], by_column=True)],
        out_specs=pl.BlockSpec((tm, tn), lambda i, j: (i, j)),
        out_shape=jax.ShapeDtypeStruct((n_rows, d), F32),
        compiler_params=_cparams("parallel", "arbitrary"),
        name="ffn_down",
    )(act, wd, x, mod)


def _proj_kernel(n_narrow, x_ref, sh_ref, sc_ref, w_ref, o16_ref, o32_ref, h_scr):
    j = pl.program_id(1)

    @pl.when(j == 0)
    def _():
        h = _rms(x_ref[...]) * (1.0 + sc_ref[...]) + sh_ref[...]
        h_scr[...] = h.astype(MXU_DTYPE)

    acc = _dot(h_scr[...], w_ref[...])

    @pl.when(j < n_narrow)
    def _():
        o16_ref[...] = acc.astype(o16_ref.dtype)

    @pl.when(j >= n_narrow)
    def _():
        o32_ref[...] = acc


def _proj(x, rows, mod, slots, w_narrow, w_wide, tm, tn):
    d = x.shape[1]
    n16, n32 = w_narrow.shape[1], w_wide.shape[1]
    assert n16 % tn == 0 and n32 % tn == 0
    n_narrow = n16 // tn
    return pl.pallas_call(
        functools.partial(_proj_kernel, n_narrow),
        grid=(rows.n_all // tm, (n16 + n32) // tn),
        in_specs=[pl.BlockSpec((tm, d), lambda i, j: (i, 0)),
                  rows.mod_spec(tm, d, slots[0]), rows.mod_spec(tm, d, slots[1]),
                  pl.BlockSpec((d, tn), lambda i, j: (0, j))],
        out_specs=[pl.BlockSpec((tm, tn), lambda i, j: (i, jnp.minimum(j, n_narrow - 1))),
                   pl.BlockSpec((tm, tn), lambda i, j: (i, jnp.maximum(j - n_narrow, 0)))],
        out_shape=[jax.ShapeDtypeStruct((rows.n_all, n16), MXU_DTYPE),
                   jax.ShapeDtypeStruct((rows.n_all, n32), F32)],
        scratch_shapes=[pltpu.VMEM((tm, d), MXU_DTYPE)],
        compiler_params=_cparams("parallel", "arbitrary"),
        name="mixer_in_proj",
    )(x, mod, mod, jnp.concatenate([w_narrow, w_wide], axis=1))


def _out_kernel(n_parts, widths, n_lat_tiles, has_ctx, *refs):
    x_ref, gt_ref, w_ref = refs[0], refs[1], refs[2]
    lat = refs[3:3 + n_parts]
    ctx = refs[3 + n_parts:3 + 2 * n_parts] if has_ctx else ()
    o_ref = refs[-1]

    def run(parts):
        acc = None
        off = 0
        for p, wd in zip(parts, widths):
            t = _dot(p[...], w_ref[off:off + wd, :])
            acc = t if acc is None else acc + t
            off += wd
        o_ref[...] = x_ref[...] + gt_ref[...] * acc

    if has_ctx:
        i = pl.program_id(0)
        pl.when(i < n_lat_tiles)(lambda: run(lat))
        pl.when(i >= n_lat_tiles)(lambda: run(ctx))
    else:
        run(lat)


def _out_proj(x, rows, mod, slot, w, lat_parts, ctx_parts, tm):
    d = x.shape[1]
    has_ctx = ctx_parts is not None
    n_rows = rows.n_all if has_ctx else rows.n_lat
    n_lat_tiles = rows.n_lat // tm
    widths = tuple(p.shape[1] for p in lat_parts)
    specs = [pl.BlockSpec((tm, d), lambda i: (i, 0)), rows.mod_spec(tm, d, slot),
             pl.BlockSpec(w.shape, lambda i: (0, 0))]
    specs += [pl.BlockSpec((tm, wd), lambda i: (jnp.minimum(i, n_lat_tiles - 1), 0)) for wd in widths]
    args = [x, mod, w] + list(lat_parts)
    if has_ctx:
        specs += [pl.BlockSpec((tm, wd), lambda i: (jnp.maximum(i - n_lat_tiles, 0), 0)) for wd in widths]
        args += list(ctx_parts)
    return pl.pallas_call(
        functools.partial(_out_kernel, len(widths), widths, n_lat_tiles, has_ctx),
        grid=(n_rows // tm,),
        in_specs=specs,
        out_specs=pl.BlockSpec((tm, d), lambda i: (i, 0)),
        out_shape=jax.ShapeDtypeStruct((n_rows, d), F32),
        compiler_params=_cparams("parallel"),
        name="mixer_out_proj",
    )(*args)


def _scan_tables(chunk):
    c = chunk
    t = np.arange(c)[:, None]
    u = np.arange(c)[None, :]
    tri = (u <= t).astype(np.float32)
    x = t ^ u
    lvl = np.zeros((c, c), np.int32)
    nz = x > 0
    lvl[nz] = int(np.log2(c)) - np.floor(np.log2(x[nz])).astype(np.int32)
    return tri, lvl[:c // 2, :c // 2]


def _level_ref(cum, m):
    c, dk = cum.shape
    if 2 * m >= SUBLANES:
        x = cum.reshape(c // (2 * m), 2 * m, dk)
        return jnp.broadcast_to(x[:, m - 1:m, :], x.shape).reshape(c, dk)
    x = cum.reshape(c // SUBLANES, SUBLANES, dk)
    sub = lax.broadcasted_iota(jnp.int32, x.shape, 1)
    lo, hi = (jnp.broadcast_to(x[:, r:r + 1, :], x.shape) for r in (m - 1, 2 * m + m - 1))
    return jnp.where(sub < 2 * m, lo, hi).reshape(c, dk)


def _chunk_local(q, ks, vb, gs, tri, lvl):
    c, dk = q.shape
    n_levels = int(np.log2(c))
    g_f, g_b = (g * LOG2E for g in gs)
    pieces = []
    for g in (g_f, g_b):
        hi = g.astype(MXU_DTYPE)
        pieces += [hi, (g - hi.astype(F32)).astype(MXU_DTYPE)]
    ex = _dot(tri, jnp.concatenate(pieces, axis=1))
    cum_f = ex[:, :dk] + ex[:, dk:2 * dk]
    cum_b = ex[:, 2 * dk:3 * dk] + ex[:, 3 * dk:]
    tot_f, tot_b = cum_f[c - 1:c], cum_b[c - 1:c]
    a_b = cum_b - g_b
    same_k = ks[0] is ks[1]
    k_sum = ks[0] * 2.0 if same_k else ks[0] + ks[1]
    h = c // 2

    def diag_blocks(zq, zk):
        zq, zk = zq.astype(MXU_DTYPE), zk.astype(MXU_DTYPE)
        return [_dot_nt(zq[:h], zk[:h]), _dot_nt(zq[h:], zk[h:])]

    attn = [jnp.where(lvl == 0, a, 0.0) for a in diag_blocks(q, k_sum)]
    row = lax.broadcasted_iota(jnp.int32, (c, 1), 0)
    for li in range(n_levels):
        m = c >> (li + 1)
        second = (row & m) != 0
        k_m = ks[0] if same_k else jnp.where(second, ks[1], ks[0])
        if m == 1:
            zq = q * jnp.exp2(jnp.where(second, g_f, g_b))
            zk = k_m
        else:
            d_f = cum_f - _level_ref(cum_f, m)
            d_b = a_b - _level_ref(cum_b, m)
            zq = q * jnp.exp2(jnp.where(second, d_f, -d_b))
            zk = k_m * jnp.exp2(jnp.where(second, d_b, -d_f))
        if m == h:
            zq, zk = zq.astype(MXU_DTYPE), zk.astype(MXU_DTYPE)
            lower, upper = _dot_nt(zq[h:], zk[:h]), _dot_nt(zq[:h], zk[h:])
        else:
            attn = [jnp.where(lvl == li + 1, a, old) for a, old in zip(diag_blocks(zq, zk), attn)]
    attn = jnp.concatenate([jnp.concatenate([attn[0], upper], axis=1),
                            jnp.concatenate([lower, attn[1]], axis=1)], axis=0)
    o = _dot(attn.astype(MXU_DTYPE), vb)
    fwd = ((q * jnp.exp2(cum_f)).astype(MXU_DTYPE), (ks[0] * jnp.exp2(tot_f - cum_f)).astype(MXU_DTYPE),
           jnp.exp2(tot_f))
    bwd = ((q * jnp.exp2(tot_b - a_b)).astype(MXU_DTYPE), (ks[1] * jnp.exp2(a_b)).astype(MXU_DTYPE),
           jnp.exp2(tot_b))
    return o, fwd, bwd


def _bidir_scan(segments, chunk, gates, finish, tabs, scr):
    tri, lvl = tabs
    acc, vt_s, qh_s, kh_s, dec_s, st_s = scr
    n_seg = [n_rows // chunk for n_rows, _ in segments]
    n_all = sum(n_seg)

    def rows_of(j):
        return pl.ds(pl.multiple_of(j * chunk, chunk), chunk)

    base = 0
    for (n_rows, seg), n in zip(segments, n_seg):
        def local(ci, _, seg=seg, base=base):
            q, ks, v, gs = gates(seg, rows_of(ci))
            j = base + ci
            vb = v.astype(MXU_DTYPE)
            vt_s[j] = vb.T
            o, fwd, bwd = _chunk_local(q, ks, vb, gs, tri, lvl)
            for d, (qh, kh, dec) in enumerate((fwd, bwd)):
                qh_s[d, rows_of(j), :] = qh
                kh_s[d, rows_of(j), :] = kh
                dec_s[d, pl.ds(pl.multiple_of(j * SUBLANES, SUBLANES), SUBLANES), :] = jnp.broadcast_to(
                    dec, (SUBLANES, dec.shape[1]))
            acc[rows_of(j), :] = o
            return 0
        lax.fori_loop(0, n, local, 0)
        base += n

    st_s[...] = jnp.zeros_like(st_s)
    n_first = n_seg[0]

    def carry(i, _):
        j_b = jnp.where(i < n_first, n_first - 1 - i, n_all + n_first - 1 - i)
        for d, j in enumerate((i, j_b)):
            st = st_s[d]
            acc[rows_of(j), :] += _dot_nt(qh_s[d, rows_of(j), :], st.astype(MXU_DTYPE))
            dec = dec_s[d, pl.ds(pl.multiple_of(j * SUBLANES, SUBLANES), 1), :]
            st_s[d] = st * dec + _dot(vt_s[j], kh_s[d, rows_of(j), :])
        return 0
    lax.fori_loop(0, n_all, carry, 0, unroll=True)

    base = 0
    for (n_rows, seg), n in zip(segments, n_seg):
        def done(ci, _, seg=seg, base=base):
            finish(seg, rows_of(ci), acc[rows_of(base + ci), :])
            return 0
        lax.fori_loop(0, n, done, 0)
        base += n


def _log_sigmoid(x):
    return jnp.minimum(x, 0.0) - jnp.log(1.0 + jnp.exp(-jnp.abs(x)))


def _gla_kernel(chunk, seq, ctx, scale,
                tri, lvl, w2_ref, b_ref, gain_ref,
                q_c, k_c, v_c, gr_c, lr_c, q_l, k_l, v_l, gr_l, lr_l,
                y_c, y_l, *scr):
    ins = ((q_c, k_c, v_c, lr_c, gr_c, y_c), (q_l, k_l, v_l, lr_l, gr_l, y_l))

    def gates(seg, sl):
        q_r, k_r, v_r, lr_r = ins[seg][:4]
        lr = lr_r[sl, :].astype(MXU_DTYPE)
        gs = tuple(_log_sigmoid(_dot(lr, w2_ref[d]) + b_ref[d]) * (1.0 / GLA_GATE_TAU) for d in range(2))
        k = k_r[sl, :].astype(F32)
        return q_r[sl, :].astype(F32) * scale, (k, k), v_r[sl, :], gs

    def finish(seg, sl, o):
        gr_r, y_r = ins[seg][4:]
        y_r[sl, :] = (_rms(o) * gain_ref[...] * _silu(gr_r[sl, :].astype(F32))).astype(y_r.dtype)

    _bidir_scan([(ctx, 0), (seq, 1)], chunk, gates, finish, (tri[...], lvl[...]), scr)


def _hgrn_kernel(chunk, seq, ctx, layer_j,
                 tri, lvl, lbl_ref, gain_ref,
                 q_c, i_c, hg_c, ff_c, fb_c, q_l, i_l, hg_l, ff_l, fb_l,
                 y_c, y_l, *scr):
    ins = ((q_c, (ff_c, fb_c), i_c, hg_c, y_c), (q_l, (ff_l, fb_l), i_l, hg_l, y_l))
    lbs = []
    for direction in range(2):
        logits = lbl_ref[direction]
        pe = jnp.exp(logits - jnp.max(logits, axis=0, keepdims=True))
        p = pe / jnp.sum(pe, axis=0, keepdims=True)
        lbs.append(jnp.sum(p[0:layer_j + 1], axis=0, keepdims=True) - p[0:1])

    def gates(seg, sl):
        q_r, f_rs, i_r = ins[seg][:3]
        ks, gs = [], []
        for d in range(2):
            lb = lbs[d]
            x = f_rs[d][sl, :]
            en = jnp.exp(-jnp.abs(x))
            inv = 1.0 / (1.0 + en)
            pos = x >= 0.0
            sig_p = jnp.where(pos, inv, en * inv)
            sig_n = jnp.where(pos, en * inv, inv)
            gs.append(jnp.log(jnp.maximum(lb + (1.0 - lb) * sig_p, F_FLOOR)))
            ks.append((1.0 - lb) * sig_n)
        return _silu(q_r[sl, :].astype(F32)), ks, i_r[sl, :], gs

    def finish(seg, sl, o):
        hg_r, y_r = ins[seg][3:]
        y_r[sl, :] = (_rms(o) * gain_ref[...] * _silu(hg_r[sl, :].astype(F32))).astype(y_r.dtype)

    _bidir_scan([(ctx, 0), (seq, 1)], chunk, gates, finish, (tri[...], lvl[...]), scr)


def _scan_scratch(chunk, n_rows, dk, dv):
    n_chunks = n_rows // chunk
    return [pltpu.VMEM((n_rows, dv), F32), pltpu.VMEM((n_chunks, dv, chunk), MXU_DTYPE),
            pltpu.VMEM((2, n_rows, dk), MXU_DTYPE), pltpu.VMEM((2, n_rows, dk), MXU_DTYPE),
            pltpu.VMEM((2, n_chunks * SUBLANES, dk), F32), pltpu.VMEM((2, dv, dk), F32)]


def _scan_chunk(rows):
    return _row_tile(rows.seq, rows.ctx, SCAN_CHUNK)


def _scan_table_args(chunk):
    tri, lvl = _scan_tables(chunk)
    args = [jnp.asarray(tri, MXU_DTYPE), jnp.asarray(lvl)]
    specs = [pl.BlockSpec(a.shape, lambda b, h: (0, 0)) for a in args]
    return args, specs


def _seg_inputs(rows, cols):
    ctx_base = rows.n_lat // rows.ctx
    specs, arrays = [], []
    for n_rows, base in ((rows.ctx, ctx_base), (rows.seq, 0)):
        for arr, w, c0, per_head in cols:
            specs.append(pl.BlockSpec(
                (n_rows, w), lambda b, h, base=base, cb=c0 // w, ph=int(per_head): (base + b, cb + ph * h)))
            arrays.append(arr)
    return specs, arrays


def _scan_call(body, name, rows, n_heads, dk, dv, chunk, param_specs, params, seg_cols):
    targs, tspecs = _scan_table_args(chunk)
    seg_specs, seg_arrays = _seg_inputs(rows, seg_cols)
    width = n_heads * dv
    y_c, y_l = pl.pallas_call(
        body,
        grid=(rows.batch, n_heads),
        in_specs=tspecs + param_specs + seg_specs,
        out_specs=[pl.BlockSpec((rows.ctx, dv), lambda b, h: (b, h)),
                   pl.BlockSpec((rows.seq, dv), lambda b, h: (b, h))],
        out_shape=[jax.ShapeDtypeStruct((rows.batch * rows.ctx, width), MXU_DTYPE),
                   jax.ShapeDtypeStruct((rows.n_lat, width), MXU_DTYPE)],
        scratch_shapes=_scan_scratch(chunk, rows.ctx + rows.seq, dk, dv),
        compiler_params=_cparams("parallel", "parallel"),
        name=name,
    )(*targs, *params, *seg_arrays)
    return y_l, y_c


def _gla_mixer(p16, p32, rows, c16, c32, w2p, b2, gain, dk, dv):
    chunk = _scan_chunk(rows)
    u = LANES
    param_specs = [pl.BlockSpec((2, u, dk), lambda b, h: (0, 0, h)),
                   pl.BlockSpec((2, 1, dk), lambda b, h: (0, 0, h)),
                   pl.BlockSpec((1, dv), lambda b, h: (0, 0))]
    seg_cols = [(p16, dk, c16["gq"], True), (p16, dk, c16["gk"], True), (p16, dv, c16["gv"], True),
                (p16, dv, c16["gr"], True), (p32, u, c32["glr"], False)]
    return _scan_call(functools.partial(_gla_kernel, chunk, rows.seq, rows.ctx, dk ** -0.5), "gla_scan",
                      rows, GLA_HEADS, dk, dv, chunk, param_specs, [w2p, b2, gain], seg_cols)


def _hgrn_mixer(p16, p32, rows, c16, c32, lb_logits, gain, layer_j, n_heads, dk):
    chunk = _scan_chunk(rows)
    n_even = lb_logits.shape[1]
    param_specs = [pl.BlockSpec((2, n_even, dk), lambda b, h: (0, 0, h)),
                   pl.BlockSpec((1, dk), lambda b, h: (0, 0))]
    seg_cols = [(p16, dk, c16["hq"], True), (p16, dk, c16["hi"], True), (p16, dk, c16["hg"], True),
                (p32, dk, c32["hf"], True), (p32, dk, c32["hf"] + n_heads * dk, True)]
    return _scan_call(functools.partial(_hgrn_kernel, chunk, rows.seq, rows.ctx, layer_j), "hgrn_scan",
                      rows, n_heads, dk, dk, chunk, param_specs, [lb_logits, gain], seg_cols)


def _qk_kernel(n_q, n_kv, p_ref, cos_ref, sin_ref, qg_ref, kg_ref, q_ref, k_ref):
    hd = ATTN_HEAD_DIM
    cos, sin = cos_ref[...], sin_ref[...]

    def rope(a):
        return a * cos + pltpu.roll(a, hd // 2, axis=1) * sin

    q_gain = qg_ref[...] * (hd ** -0.5 * LOG2E)
    for h in range(n_q):
        q_ref[:, h * hd:(h + 1) * hd] = rope(_rms(p_ref[:, h * hd:(h + 1) * hd]) * q_gain).astype(q_ref.dtype)
    for h in range(n_kv):
        a = _rms(p_ref[:, (n_q + h) * hd:(n_q + h + 1) * hd]) * kg_ref[...]
        k_ref[:, h * hd:(h + 1) * hd] = rope(a).astype(k_ref.dtype)


def _qk_prep(p, rows, cos, sin, qg, kg, n_q, n_kv, tm):
    hd = ATTN_HEAD_DIM
    n_lat_tiles = rows.n_lat // tm
    per_seq = rows.seq // tm
    tab = pl.BlockSpec((tm, hd), lambda i: (jnp.where(i < n_lat_tiles, i % per_seq, per_seq), 0))
    return pl.pallas_call(
        functools.partial(_qk_kernel, n_q, n_kv),
        grid=(rows.n_all // tm,),
        in_specs=[pl.BlockSpec((tm, p.shape[1]), lambda i: (i, 0)), tab, tab,
                  pl.BlockSpec((1, hd), lambda i: (0, 0)), pl.BlockSpec((1, hd), lambda i: (0, 0))],
        out_specs=[pl.BlockSpec((tm, n_q * hd), lambda i: (i, 0)),
                   pl.BlockSpec((tm, n_kv * hd), lambda i: (i, 0))],
        out_shape=[jax.ShapeDtypeStruct((rows.n_all, n_q * hd), MXU_DTYPE),
                   jax.ShapeDtypeStruct((rows.n_all, n_kv * hd), MXU_DTYPE)],
        compiler_params=_cparams("parallel"),
        name="qk_norm_rope",
    )(p, cos, sin, qg, kg)


def _attn_kernel(n_kv_sets, kv_heads, q_ref, *refs):
    kv = refs[:2 * n_kv_sets]
    o_ref, s_scr = refs[2 * n_kv_sets:]
    hd = ATTN_HEAD_DIM
    n_heads = kv_heads * ATTN_GROUP
    bounds = np.cumsum([0] + [kv[2 * i].shape[0] for i in range(n_kv_sets)])

    def scores(g):
        q = q_ref[:, g * hd:(g + 1) * hd]
        c0 = (g // ATTN_GROUP) * hd
        for i in range(n_kv_sets):
            s_scr[g % 2, :, bounds[i]:bounds[i + 1]] = _dot_nt(q, kv[2 * i][:, c0:c0 + hd])

    scores(0)
    for g in range(n_heads):
        if g + 1 < n_heads:
            scores(g + 1)
        s = s_scr[g % 2]
        p = jnp.exp2(s - jnp.max(s, axis=-1, keepdims=True))
        den = jnp.sum(p, axis=-1, keepdims=True)
        pb = p.astype(MXU_DTYPE)
        c0 = (g // ATTN_GROUP) * hd
        acc = functools.reduce(jnp.add, [_dot(pb[:, bounds[i]:bounds[i + 1]], kv[2 * i + 1][:, c0:c0 + hd])
                                         for i in range(n_kv_sets)])
        o_ref[:, g * hd:(g + 1) * hd] = (acc / den).astype(o_ref.dtype)


def _attention(q, k, v, rows, n_kv, tq, ctx_queries):
    hd = ATTN_HEAD_DIM
    kvh = ATTN_KV_PER_STEP if n_kv % ATTN_KV_PER_STEP == 0 else 1
    kw = kvh * hd
    gw = ATTN_GROUP * kw
    ctx_base = rows.n_lat // rows.ctx
    ctx_kv = pl.BlockSpec((rows.ctx, kw), lambda b, h, i: (ctx_base + b, h))
    if ctx_queries:
        n_q, per = rows.batch * rows.ctx, 1
        tq = rows.ctx
        q_spec = pl.BlockSpec((tq, gw), lambda b, h, i: (ctx_base + b, h))
        kv_specs, kv_args, n_keys = [ctx_kv, ctx_kv], [k, v], rows.ctx
    else:
        n_q, per = rows.n_lat, rows.seq // tq
        q_spec = pl.BlockSpec((tq, gw), lambda b, h, i: (b * per + i, h))
        lat_kv = pl.BlockSpec((rows.seq, kw), lambda b, h, i: (b, h))
        kv_specs, kv_args, n_keys = [lat_kv, lat_kv, ctx_kv, ctx_kv], [k, v, k, v], rows.seq + rows.ctx
    return pl.pallas_call(
        functools.partial(_attn_kernel, len(kv_args) // 2, kvh),
        grid=(rows.batch, n_kv // kvh, per),
        in_specs=[q_spec] + kv_specs,
        out_specs=pl.BlockSpec((tq, gw), lambda b, h, i: (b * per + i, h)),
        out_shape=jax.ShapeDtypeStruct((n_q, n_kv * ATTN_GROUP * hd), MXU_DTYPE),
        scratch_shapes=[pltpu.VMEM((2, tq, n_keys), F32)],
        compiler_params=_cparams("parallel", "parallel", "arbitrary"),
        name="gqa_ctx" if ctx_queries else "gqa_latent",
    )(q, *kv_args)


def _rope_perm():
    q = ATTN_HEAD_DIM // 4
    return np.concatenate([np.arange(0, q), np.arange(2 * q, 3 * q), np.arange(q, 2 * q), np.arange(3 * q, 4 * q)])


def _permute_heads(a, n_heads):
    lead = a.shape[:-1]
    return a.reshape(*lead, n_heads, ATTN_HEAD_DIM)[..., _rope_perm()].reshape(*lead, n_heads * ATTN_HEAD_DIM)


def _rope_tables(seq, pad_rows):
    hd = ATTN_HEAD_DIM
    half = hd // 2
    pos = jnp.arange(seq)
    inv_freq = ROPE_THETA ** (-jnp.arange(0, half, 2, dtype=F32) / half)
    ang = jnp.stack([pos // GRID_W, pos % GRID_W], axis=-1).astype(F32)[:, :, None] * inv_freq
    ang = ang.reshape(seq, half)
    cos = jnp.concatenate([jnp.cos(ang), jnp.cos(ang)], axis=-1)
    sin = jnp.concatenate([-jnp.sin(ang), jnp.sin(ang)], axis=-1)
    cos = jnp.concatenate([cos, jnp.ones((pad_rows, hd), F32)], axis=0)
    sin = jnp.concatenate([sin, jnp.zeros((pad_rows, hd), F32)], axis=0)
    return cos, sin


def _final_kernel(x_ref, g_ref, o_ref):
    o_ref[...] = _rms(x_ref[...]) * g_ref[...]


def _final_norm(x, gain, tm):
    n, d = x.shape
    return pl.pallas_call(
        _final_kernel,
        grid=(n // tm,),
        in_specs=[pl.BlockSpec((tm, d), lambda i: (i, 0)), pl.BlockSpec((1, d), lambda i: (0, 0))],
        out_specs=pl.BlockSpec((tm, d), lambda i: (i, 0)),
        out_shape=jax.ShapeDtypeStruct((n, d), F32),
        compiler_params=_cparams("parallel"),
        name="final_norm",
    )(x, gain.reshape(1, d))


def _even_layout(d):
    gla_w = d // 2
    gla_qk = gla_w // 2
    hgrn_w = d // 2
    names = ["gq", "gk", "gv", "gr", "glr", "hq", "hf", "hi", "hg"]
    widths = [gla_qk, gla_qk, gla_w, gla_w, 2 * GLA_GATE_RANK, hgrn_w, 2 * hgrn_w, hgrn_w, hgrn_w]
    src = dict(zip(names, zip((int(o) for o in np.cumsum([0] + widths[:-1])), widths)))
    narrow = {k: src[k] for k in ("gq", "gk", "gv", "gr", "hq", "hi", "hg")}
    wide = {k: src[k] for k in ("glr", "hf")}
    return narrow, wide


def _gather_cols(w, group, pad_to):
    parts, offs, off = [], {}, 0
    for name, (c0, width) in group.items():
        padded = -(-width // pad_to) * pad_to
        parts.append(w[:, c0:c0 + width])
        if padded > width:
            parts.append(jnp.zeros((w.shape[0], padded - width), w.dtype))
        offs[name] = off
        off += padded
    return jnp.concatenate(parts, axis=1), offs


def kernel(x, c, ctx, c_ctx, w_mod, b_mod, ffn_w_gate, ffn_w_up, ffn_w_down, mix_ab_w_in, gla_gate_w2, gla_gate_b,
           gla_norm_g, hgrn_lb_logits, hgrn_norm_g, mix_ab_w_out, attn_w_in, attn_q_norm_g, attn_k_norm_g,
           attn_w_out, final_norm_g):
    batch, seq, d = x.shape
    n_ctx = ctx.shape[1]
    depth = w_mod.shape[0]
    rows = _Rows(batch, seq, n_ctx)
    tm = _row_tile(seq, batch * n_ctx, 512)
    tm_mm = _row_tile(seq, batch * n_ctx, 1024)
    bf = MXU_DTYPE
    wg_all, wu_all, wd_all = ffn_w_gate, ffn_w_up, ffn_w_down.astype(bf)

    gla_w = d // 2
    gla_dv = gla_w // GLA_HEADS
    gla_dk = gla_dv // 2
    hgrn_w = d // 2
    hgrn_heads = hgrn_w // HGRN_EXPAND
    n_q_heads = d // ATTN_HEAD_DIM
    n_kv_heads = n_q_heads // ATTN_GROUP

    cond = jnp.zeros((MOD_ROWS, d), F32).at[:batch].set(c).at[batch].set(c_ctx)
    mod = _modulation(cond, w_mod, b_mod)
    mod = mod.reshape(depth, MOD_ROWS, N_MOD, d).transpose(0, 2, 1, 3).reshape(depth, N_MOD * MOD_ROWS, 1, d)

    narrow, wide = _even_layout(d)
    cos, sin = _rope_tables(seq, tm)

    xs = jnp.concatenate([x.reshape(batch * seq, d), ctx.reshape(batch * n_ctx, d)], axis=0)
    for layer in range(depth):
        last = layer == depth - 1
        j = layer // 2
        m = mod[layer]
        xs = _ffn(xs, rows.n_all, rows, m, (0, 1, 2), wg_all, wu_all, wd_all, (layer, 0), tm_mm)
        if layer % 2 == 0:
            w16, c16 = _gather_cols(mix_ab_w_in[j], narrow, MXU_N)
            w32, c32 = _gather_cols(mix_ab_w_in[j], wide, MXU_N)
            tn = _col_tile(np.gcd(w16.shape[1], w32.shape[1]))
            p16, p32 = _proj(xs, rows, m, (3, 4), w16.astype(bf), w32.astype(bf), tm_mm, tn)
            w2p = jnp.zeros((2, LANES, gla_w // 2), F32)
            for direction in range(2):
                r0 = direction * GLA_GATE_RANK
                w2p = w2p.at[direction, r0:r0 + GLA_GATE_RANK].set(gla_gate_w2[j, direction])
            ya_l, ya_c = _gla_mixer(p16, p32, rows, c16, c32, w2p.astype(bf), gla_gate_b[j].reshape(2, 1, -1),
                                    gla_norm_g[j].reshape(1, -1), gla_dk, gla_dv)
            yb_l, yb_c = _hgrn_mixer(p16, p32, rows, c16, c32, hgrn_lb_logits, hgrn_norm_g[j].reshape(1, -1), j,
                                     hgrn_heads, HGRN_EXPAND)
            lat_parts, ctx_parts = [ya_l, yb_l], [ya_c, yb_c]
            w_out = mix_ab_w_out[j].astype(bf)
        else:
            n_qk = (n_q_heads + n_kv_heads) * ATTN_HEAD_DIM
            w_v = attn_w_in[j][:, n_qk:].astype(bf)
            w_qk = _permute_heads(attn_w_in[j][:, :n_qk], n_q_heads + n_kv_heads).astype(bf)
            v, p_qk = _proj(xs, rows, m, (3, 4), w_v, w_qk, tm_mm, _col_tile(np.gcd(w_v.shape[1], n_qk)))
            q, k = _qk_prep(p_qk, rows, cos, sin, _permute_heads(attn_q_norm_g[j].reshape(1, -1), 1),
                            _permute_heads(attn_k_norm_g[j].reshape(1, -1), 1), n_q_heads, n_kv_heads, tm)
            lat_parts = [_attention(q, k, v, rows, n_kv_heads, _row_tile(seq, seq, 256), False)]
            ctx_parts = None if last else [_attention(q, k, v, rows, n_kv_heads, n_ctx, True)]
            w_out = attn_w_out[j].astype(bf)
        if last:
            ctx_parts = None
        xs = _out_proj(xs, rows, m, 5, w_out, lat_parts, ctx_parts, tm)
        n_rows = rows.n_lat if last else rows.n_all
        xs = _ffn(xs, n_rows, rows, m, (6, 7, 8), wg_all, wu_all, wd_all, (layer, 1), tm_mm)
    out = _final_norm(xs, final_norm_g, tm)
    return out.reshape(batch, seq, d)
```

```python
import functools

import jax
import jax.numpy as jnp
import numpy as np
from jax import lax
from jax.experimental import pallas as pl
from jax.experimental.pallas import tpu as pltpu

F32 = jnp.float32
MXU_DTYPE = jnp.bfloat16

EPS = 1e-6
F_FLOOR = 1e-30
N_MOD = 9
GRID_W = 64
GLA_HEADS = 4
GLA_GATE_RANK = 16
GLA_GATE_TAU = 16.0
HGRN_EXPAND = 128
ATTN_HEAD_DIM = 128
ATTN_GROUP = 4
ATTN_KV_PER_STEP = 4
ROPE_THETA = 10000.0

LANES = 128
MXU_N = 256
VMEM_LIMIT_BYTES = 56 * 1024 * 1024

SCAN_CHUNK = 256
SUBLANES = 8
LOG2E = 1.4426950408889634
MOD_ROWS = 16


def _cparams(*sem):
    return pltpu.CompilerParams(dimension_semantics=sem, vmem_limit_bytes=VMEM_LIMIT_BYTES)


def _sigmoid(x):
    return 1.0 / (1.0 + jnp.exp(-x))


def _silu(x):
    return x * _sigmoid(x)


def _rms(x):
    return x * lax.rsqrt(jnp.mean(x * x, axis=-1, keepdims=True) + EPS)


def _dot(a, b):
    return jnp.dot(a, b, preferred_element_type=F32)


def _dot_nt(a, b):
    return lax.dot_general(a, b, (((1,), (1,)), ((), ())), preferred_element_type=F32)


def _dot_tn(a, b):
    return lax.dot_general(a, b, (((0,), (0,)), ((), ())), preferred_element_type=F32)


def _row_tile(n_lat, n_ctx_total, cap):
    t = cap
    while n_lat % t or n_ctx_total % t:
        t //= 2
    return t


def _col_tile(n, cap=1024):
    return max(t for t in range(MXU_N, cap + 1, MXU_N) if n % t == 0)


def _mod_kernel(c_ref, w_ref, b_ref, o_ref):
    s = _silu(c_ref[...]).astype(MXU_DTYPE)
    o_ref[...] = _dot(s, w_ref[...].astype(MXU_DTYPE)) + b_ref[...]


def _modulation(cond, w_mod, b_mod):
    depth, d, n = w_mod.shape
    tn = 1024
    return pl.pallas_call(
        _mod_kernel,
        grid=(depth, n // tn),
        in_specs=[pl.BlockSpec((MOD_ROWS, d), lambda l, j: (0, 0)),
                  pl.BlockSpec((None, d, tn), lambda l, j: (l, 0, j)),
                  pl.BlockSpec((None, 1, tn), lambda l, j: (l, 0, j))],
        out_specs=pl.BlockSpec((None, MOD_ROWS, tn), lambda l, j: (l, 0, j)),
        out_shape=jax.ShapeDtypeStruct((depth, MOD_ROWS, n), F32),
        compiler_params=_cparams("parallel", "parallel"),
        name="modulation",
    )(cond, w_mod, b_mod.reshape(depth, 1, n))


class _Rows:
    def __init__(self, batch, seq, ctx):
        self.batch, self.seq, self.ctx = batch, seq, ctx
        self.n_lat = batch * seq
        self.n_all = self.n_lat + batch * ctx

    def mod_spec(self, tm, width, slot, by_column=False):
        seq, batch = self.seq, self.batch

        def row(i):
            return slot * MOD_ROWS + jnp.minimum((i * tm) // seq, batch)
        if by_column:
            return pl.BlockSpec((None, 1, width), lambda i, j: (row(i), 0, j))
        return pl.BlockSpec((None, 1, width), lambda i, *_: (row(i), 0, 0))


def _ffn_up_kernel(x_ref, sh_ref, sc_ref, wg_ref, wu_ref, o_ref, h_scr):
    @pl.when(pl.program_id(1) == 0)
    def _():
        h = _rms(x_ref[...]) * (1.0 + sc_ref[...]) + sh_ref[...]
        h_scr[...] = h.astype(MXU_DTYPE)

    h = h_scr[...]
    o_ref[...] = (_silu(_dot(h, wg_ref[...])) * _dot(h, wu_ref[...])).astype(o_ref.dtype)


def _ffn_down_kernel(a_ref, wd_ref, x_ref, gt_ref, o_ref):
    o_ref[...] = x_ref[...] + 0.5 * gt_ref[...] * _dot(a_ref[...], wd_ref[...])


def _ffn(x, n_rows, rows, mod, slots, wg, wu, wd, which, tm):
    d = x.shape[1]
    dff = wg.shape[3]
    layer, half = which
    tf = _col_tile(dff, 512)
    act = pl.pallas_call(
        _ffn_up_kernel,
        grid=(n_rows // tm, dff // tf),
        in_specs=[pl.BlockSpec((tm, d), lambda i, j: (i, 0)),
                  rows.mod_spec(tm, d, slots[0]), rows.mod_spec(tm, d, slots[1]),
                  pl.BlockSpec((None, None, d, tf), lambda i, j: (layer, half, 0, j)),
                  pl.BlockSpec((None, None, d, tf), lambda i, j: (layer, half, 0, j))],
        out_specs=pl.BlockSpec((tm, tf), lambda i, j: (i, j)),
        out_shape=jax.ShapeDtypeStruct((n_rows, dff), MXU_DTYPE),
        scratch_shapes=[pltpu.VMEM((tm, d), MXU_DTYPE)],
        compiler_params=_cparams("parallel", "arbitrary"),
        name="ffn_up",
    )(x, mod, mod, wg, wu)
    tn = _col_tile(d, 512)
    return pl.pallas_call(
        _ffn_down_kernel,
        grid=(n_rows // tm, d // tn),
        in_specs=[pl.BlockSpec((tm, dff), lambda i, j: (i, 0)),
                  pl.BlockSpec((None, None, dff, tn), lambda i, j: (layer, half, 0, j)),
                  pl.BlockSpec((tm, tn), lambda i, j: (i, j)),
                  rows.mod_spec(tm, tn, slots[2], by_column=True)],
        out_specs=pl.BlockSpec((tm, tn), lambda i, j: (i, j)),
        out_shape=jax.ShapeDtypeStruct((n_rows, d), F32),
        compiler_params=_cparams("parallel", "arbitrary"),
        name="ffn_down",
    )(act, wd, x, mod)


def _proj_kernel(x_ref, sh_ref, sc_ref, w_ref, o_ref, h_scr):
    @pl.when(pl.program_id(1) == 0)
    def _():
        h = _rms(x_ref[...]) * (1.0 + sc_ref[...]) + sh_ref[...]
        h_scr[...] = h.astype(MXU_DTYPE)

    o_ref[...] = _dot(h_scr[...], w_ref[...]).astype(o_ref.dtype)


def _proj(x, rows, mod, slots, w, tm, tn, out_dtype=F32):
    d = x.shape[1]
    n = w.shape[1]
    return pl.pallas_call(
        _proj_kernel,
        grid=(rows.n_all // tm, n // tn),
        in_specs=[pl.BlockSpec((tm, d), lambda i, j: (i, 0)),
                  rows.mod_spec(tm, d, slots[0]), rows.mod_spec(tm, d, slots[1]),
                  pl.BlockSpec((d, tn), lambda i, j: (0, j))],
        out_specs=pl.BlockSpec((tm, tn), lambda i, j: (i, j)),
        out_shape=jax.ShapeDtypeStruct((rows.n_all, n), out_dtype),
        scratch_shapes=[pltpu.VMEM((tm, d), MXU_DTYPE)],
        compiler_params=_cparams("parallel", "arbitrary"),
        name="mixer_in_proj",
    )(x, mod, mod, w)


def _out_kernel(n_parts, widths, n_lat_tiles, has_ctx, *refs):
    x_ref, gt_ref, w_ref = refs[0], refs[1], refs[2]
    lat = refs[3:3 + n_parts]
    ctx = refs[3 + n_parts:3 + 2 * n_parts] if has_ctx else ()
    o_ref = refs[-1]

    def run(parts):
        acc = None
        off = 0
        for p, wd in zip(parts, widths):
            t = _dot(p[...], w_ref[off:off + wd, :])
            acc = t if acc is None else acc + t
            off += wd
        o_ref[...] = x_ref[...] + gt_ref[...] * acc

    if has_ctx:
        i = pl.program_id(0)
        pl.when(i < n_lat_tiles)(lambda: run(lat))
        pl.when(i >= n_lat_tiles)(lambda: run(ctx))
    else:
        run(lat)


def _out_proj(x, rows, mod, slot, w, lat_parts, ctx_parts, tm):
    d = x.shape[1]
    has_ctx = ctx_parts is not None
    n_rows = rows.n_all if has_ctx else rows.n_lat
    n_lat_tiles = rows.n_lat // tm
    widths = tuple(p.shape[1] for p in lat_parts)
    specs = [pl.BlockSpec((tm, d), lambda i: (i, 0)), rows.mod_spec(tm, d, slot),
             pl.BlockSpec(w.shape, lambda i: (0, 0))]
    specs += [pl.BlockSpec((tm, wd), lambda i: (jnp.minimum(i, n_lat_tiles - 1), 0)) for wd in widths]
    args = [x, mod, w] + list(lat_parts)
    if has_ctx:
        specs += [pl.BlockSpec((tm, wd), lambda i: (jnp.maximum(i - n_lat_tiles, 0), 0)) for wd in widths]
        args += list(ctx_parts)
    return pl.pallas_call(
        functools.partial(_out_kernel, len(widths), widths, n_lat_tiles, has_ctx),
        grid=(n_rows // tm,),
        in_specs=specs,
        out_specs=pl.BlockSpec((tm, d), lambda i: (i, 0)),
        out_shape=jax.ShapeDtypeStruct((n_rows, d), F32),
        compiler_params=_cparams("parallel"),
        name="mixer_out_proj",
    )(*args)


def _scan_tables(chunk):
    c = chunk
    t = np.arange(c)[:, None]
    u = np.arange(c)[None, :]
    tri = (u <= t).astype(np.float32)
    x = t ^ u
    lvl = np.zeros((c, c), np.int32)
    nz = x > 0
    lvl[nz] = int(np.log2(c)) - np.floor(np.log2(x[nz])).astype(np.int32)
    return tri, lvl[:c // 2, :c // 2]


def _level_ref(cum, m):
    c, dk = cum.shape
    if 2 * m >= SUBLANES:
        x = cum.reshape(c // (2 * m), 2 * m, dk)
        return jnp.broadcast_to(x[:, m - 1:m, :], x.shape).reshape(c, dk)
    x = cum.reshape(c // SUBLANES, SUBLANES, dk)
    sub = lax.broadcasted_iota(jnp.int32, x.shape, 1)
    lo, hi = (jnp.broadcast_to(x[:, r:r + 1, :], x.shape) for r in (m - 1, 2 * m + m - 1))
    return jnp.where(sub < 2 * m, lo, hi).reshape(c, dk)


def _chunk_local(q, ks, vb, gs, tri, lvl):
    c, dk = q.shape
    n_levels = int(np.log2(c))
    g_f, g_b = (g * LOG2E for g in gs)
    pieces = []
    for g in (g_f, g_b):
        hi = g.astype(MXU_DTYPE)
        pieces += [hi, (g - hi.astype(F32)).astype(MXU_DTYPE)]
    ex = _dot(tri, jnp.concatenate(pieces, axis=1))
    cum_f = ex[:, :dk] + ex[:, dk:2 * dk]
    cum_b = ex[:, 2 * dk:3 * dk] + ex[:, 3 * dk:]
    tot_f, tot_b = cum_f[c - 1:c], cum_b[c - 1:c]
    a_b = cum_b - g_b
    same_k = ks[0] is ks[1]
    k_sum = ks[0] * 2.0 if same_k else ks[0] + ks[1]
    h = c // 2

    def diag_blocks(zq, zk):
        zq, zk = zq.astype(MXU_DTYPE), zk.astype(MXU_DTYPE)
        return [_dot_nt(zq[:h], zk[:h]), _dot_nt(zq[h:], zk[h:])]

    attn = [jnp.where(lvl == 0, a, 0.0) for a in diag_blocks(q, k_sum)]
    row = lax.broadcasted_iota(jnp.int32, (c, 1), 0)
    for li in range(n_levels):
        m = c >> (li + 1)
        second = (row & m) != 0
        k_m = ks[0] if same_k else jnp.where(second, ks[1], ks[0])
        if m == 1:
            zq = q * jnp.exp2(jnp.where(second, g_f, g_b))
            zk = k_m
        else:
            d_f = cum_f - _level_ref(cum_f, m)
            d_b = a_b - _level_ref(cum_b, m)
            zq = q * jnp.exp2(jnp.where(second, d_f, -d_b))
            zk = k_m * jnp.exp2(jnp.where(second, d_b, -d_f))
        if m == h:
            zq, zk = zq.astype(MXU_DTYPE), zk.astype(MXU_DTYPE)
            lower, upper = _dot_nt(zq[h:], zk[:h]), _dot_nt(zq[:h], zk[h:])
        else:
            attn = [jnp.where(lvl == li + 1, a, old) for a, old in zip(diag_blocks(zq, zk), attn)]
    attn = jnp.concatenate([jnp.concatenate([attn[0], upper], axis=1),
                            jnp.concatenate([lower, attn[1]], axis=1)], axis=0)
    o = _dot(attn.astype(MXU_DTYPE), vb)
    fwd = ((q * jnp.exp2(cum_f)).astype(MXU_DTYPE), (ks[0] * jnp.exp2(tot_f - cum_f)).astype(MXU_DTYPE),
           jnp.exp2(tot_f))
    bwd = ((q * jnp.exp2(tot_b - a_b)).astype(MXU_DTYPE), (ks[1] * jnp.exp2(a_b)).astype(MXU_DTYPE),
           jnp.exp2(tot_b))
    return o, fwd, bwd


def _bidir_scan(segments, chunk, gates, finish, tabs, scr):
    tri, lvl = tabs
    acc, vt_s, qh_s, kh_s, dec_s, st_s = scr
    n_seg = [n_rows // chunk for n_rows, _ in segments]
    n_all = sum(n_seg)

    def rows_of(j):
        return pl.ds(pl.multiple_of(j * chunk, chunk), chunk)

    base = 0
    for (n_rows, seg), n in zip(segments, n_seg):
        def local(ci, _, seg=seg, base=base):
            q, ks, v, gs = gates(seg, rows_of(ci))
            j = base + ci
            vb = v.astype(MXU_DTYPE)
            vt_s[j] = vb.T
            o, fwd, bwd = _chunk_local(q, ks, vb, gs, tri, lvl)
            for d, (qh, kh, dec) in enumerate((fwd, bwd)):
                qh_s[d, rows_of(j), :] = qh
                kh_s[d, rows_of(j), :] = kh
                dec_s[d, pl.ds(pl.multiple_of(j * SUBLANES, SUBLANES), SUBLANES), :] = jnp.broadcast_to(
                    dec, (SUBLANES, dec.shape[1]))
            acc[rows_of(j), :] = o
            return 0
        lax.fori_loop(0, n, local, 0)
        base += n

    st_s[...] = jnp.zeros_like(st_s)
    n_first = n_seg[0]

    def carry(i, _):
        j_b = jnp.where(i < n_first, n_first - 1 - i, n_all + n_first - 1 - i)
        for d, j in enumerate((i, j_b)):
            st = st_s[d]
            acc[rows_of(j), :] += _dot_nt(qh_s[d, rows_of(j), :], st.astype(MXU_DTYPE))
            dec = dec_s[d, pl.ds(pl.multiple_of(j * SUBLANES, SUBLANES), 1), :]
            st_s[d] = st * dec + _dot(vt_s[j], kh_s[d, rows_of(j), :])
        return 0
    lax.fori_loop(0, n_all, carry, 0, unroll=True)

    base = 0
    for (n_rows, seg), n in zip(segments, n_seg):
        def done(ci, _, seg=seg, base=base):
            finish(seg, rows_of(ci), acc[rows_of(base + ci), :])
            return 0
        lax.fori_loop(0, n, done, 0)
        base += n


def _log_sigmoid(x):
    return jnp.minimum(x, 0.0) - jnp.log(1.0 + jnp.exp(-jnp.abs(x)))


def _gla_kernel(chunk, seq, ctx, scale,
                tri, lvl, w2_ref, b_ref, gain_ref,
                q_c, k_c, v_c, gr_c, lr_c, q_l, k_l, v_l, gr_l, lr_l,
                y_c, y_l, *scr):
    ins = ((q_c, k_c, v_c, lr_c, gr_c, y_c), (q_l, k_l, v_l, lr_l, gr_l, y_l))

    def gates(seg, sl):
        q_r, k_r, v_r, lr_r = ins[seg][:4]
        lr = lr_r[sl, :].astype(MXU_DTYPE)
        gs = tuple(_log_sigmoid(_dot(lr, w2_ref[d]) + b_ref[d]) * (1.0 / GLA_GATE_TAU) for d in range(2))
        k = k_r[sl, :].astype(F32)
        return q_r[sl, :].astype(F32) * scale, (k, k), v_r[sl, :], gs

    def finish(seg, sl, o):
        gr_r, y_r = ins[seg][4:]
        y_r[sl, :] = (_rms(o) * gain_ref[...] * _silu(gr_r[sl, :].astype(F32))).astype(y_r.dtype)

    _bidir_scan([(ctx, 0), (seq, 1)], chunk, gates, finish, (tri[...], lvl[...]), scr)


def _hgrn_kernel(chunk, seq, ctx, layer_j,
                 tri, lvl, lbl_ref, gain_ref,
                 q_c, i_c, hg_c, ff_c, fb_c, q_l, i_l, hg_l, ff_l, fb_l,
                 y_c, y_l, *scr):
    ins = ((q_c, (ff_c, fb_c), i_c, hg_c, y_c), (q_l, (ff_l, fb_l), i_l, hg_l, y_l))
    lbs = []
    for direction in range(2):
        logits = lbl_ref[direction]
        pe = jnp.exp(logits - jnp.max(logits, axis=0, keepdims=True))
        p = pe / jnp.sum(pe, axis=0, keepdims=True)
        lbs.append(jnp.sum(p[0:layer_j + 1], axis=0, keepdims=True) - p[0:1])

    def gates(seg, sl):
        q_r, f_rs, i_r = ins[seg][:3]
        ks, gs = [], []
        for d in range(2):
            lb = lbs[d]
            x = f_rs[d][sl, :]
            en = jnp.exp(-jnp.abs(x))
            inv = 1.0 / (1.0 + en)
            pos = x >= 0.0
            sig_p = jnp.where(pos, inv, en * inv)
            sig_n = jnp.where(pos, en * inv, inv)
            gs.append(jnp.log(jnp.maximum(lb + (1.0 - lb) * sig_p, F_FLOOR)))
            ks.append((1.0 - lb) * sig_n)
        return _silu(q_r[sl, :].astype(F32)), ks, i_r[sl, :], gs

    def finish(seg, sl, o):
        hg_r, y_r = ins[seg][3:]
        y_r[sl, :] = (_rms(o) * gain_ref[...] * _silu(hg_r[sl, :].astype(F32))).astype(y_r.dtype)

    _bidir_scan([(ctx, 0), (seq, 1)], chunk, gates, finish, (tri[...], lvl[...]), scr)


def _scan_scratch(chunk, n_rows, dk, dv):
    n_chunks = n_rows // chunk
    return [pltpu.VMEM((n_rows, dv), F32), pltpu.VMEM((n_chunks, dv, chunk), MXU_DTYPE),
            pltpu.VMEM((2, n_rows, dk), MXU_DTYPE), pltpu.VMEM((2, n_rows, dk), MXU_DTYPE),
            pltpu.VMEM((2, n_chunks * SUBLANES, dk), F32), pltpu.VMEM((2, dv, dk), F32)]


def _scan_chunk(rows):
    return _row_tile(rows.seq, rows.ctx, SCAN_CHUNK)


def _scan_table_args(chunk):
    tri, lvl = _scan_tables(chunk)
    args = [jnp.asarray(tri, MXU_DTYPE), jnp.asarray(lvl)]
    specs = [pl.BlockSpec(a.shape, lambda b, h: (0, 0)) for a in args]
    return args, specs


def _seg_inputs(rows, cols):
    ctx_base = rows.n_lat // rows.ctx
    specs, arrays = [], []
    for n_rows, base in ((rows.ctx, ctx_base), (rows.seq, 0)):
        for arr, w, c0, per_head in cols:
            specs.append(pl.BlockSpec(
                (n_rows, w), lambda b, h, base=base, cb=c0 // w, ph=int(per_head): (base + b, cb + ph * h)))
            arrays.append(arr)
    return specs, arrays


def _scan_call(body, name, rows, n_heads, dk, dv, chunk, param_specs, params, seg_cols):
    targs, tspecs = _scan_table_args(chunk)
    seg_specs, seg_arrays = _seg_inputs(rows, seg_cols)
    width = n_heads * dv
    y_c, y_l = pl.pallas_call(
        body,
        grid=(rows.batch, n_heads),
        in_specs=tspecs + param_specs + seg_specs,
        out_specs=[pl.BlockSpec((rows.ctx, dv), lambda b, h: (b, h)),
                   pl.BlockSpec((rows.seq, dv), lambda b, h: (b, h))],
        out_shape=[jax.ShapeDtypeStruct((rows.batch * rows.ctx, width), MXU_DTYPE),
                   jax.ShapeDtypeStruct((rows.n_lat, width), MXU_DTYPE)],
        scratch_shapes=_scan_scratch(chunk, rows.ctx + rows.seq, dk, dv),
        compiler_params=_cparams("parallel", "parallel"),
        name=name,
    )(*targs, *params, *seg_arrays)
    return y_l, y_c


def _gla_mixer(p16, p32, rows, c16, c32, w2p, b2, gain, dk, dv):
    chunk = _scan_chunk(rows)
    u = LANES
    param_specs = [pl.BlockSpec((2, u, dk), lambda b, h: (0, 0, h)),
                   pl.BlockSpec((2, 1, dk), lambda b, h: (0, 0, h)),
                   pl.BlockSpec((1, dv), lambda b, h: (0, 0))]
    seg_cols = [(p16, dk, c16["gq"], True), (p16, dk, c16["gk"], True), (p16, dv, c16["gv"], True),
                (p16, dv, c16["gr"], True), (p32, u, c32["glr"], False)]
    return _scan_call(functools.partial(_gla_kernel, chunk, rows.seq, rows.ctx, dk ** -0.5), "gla_scan",
                      rows, GLA_HEADS, dk, dv, chunk, param_specs, [w2p, b2, gain], seg_cols)


def _hgrn_mixer(p16, p32, rows, c16, c32, lb_logits, gain, layer_j, n_heads, dk):
    chunk = _scan_chunk(rows)
    n_even = lb_logits.shape[1]
    param_specs = [pl.BlockSpec((2, n_even, dk), lambda b, h: (0, 0, h)),
                   pl.BlockSpec((1, dk), lambda b, h: (0, 0))]
    seg_cols = [(p16, dk, c16["hq"], True), (p16, dk, c16["hi"], True), (p16, dk, c16["hg"], True),
                (p32, dk, c32["hf"], True), (p32, dk, c32["hf"] + n_heads * dk, True)]
    return _scan_call(functools.partial(_hgrn_kernel, chunk, rows.seq, rows.ctx, layer_j), "hgrn_scan",
                      rows, n_heads, dk, dk, chunk, param_specs, [lb_logits, gain], seg_cols)


def _qkv_kernel(n_q, n_kv, p_ref, cos_ref, sin_ref, qg_ref, kg_ref, q_ref, k_ref, v_ref):
    hd = ATTN_HEAD_DIM
    cos, sin = cos_ref[...], sin_ref[...]

    def rope(a):
        return a * cos + pltpu.roll(a, hd // 2, axis=1) * sin

    q_gain = qg_ref[...] * (hd ** -0.5 * LOG2E)
    for h in range(n_q):
        q_ref[:, h * hd:(h + 1) * hd] = rope(_rms(p_ref[:, h * hd:(h + 1) * hd]) * q_gain).astype(q_ref.dtype)
    for h in range(n_kv):
        a = _rms(p_ref[:, (n_q + h) * hd:(n_q + h + 1) * hd]) * kg_ref[...]
        k_ref[:, h * hd:(h + 1) * hd] = rope(a).astype(k_ref.dtype)
    v_ref[...] = p_ref[:, (n_q + n_kv) * hd:].astype(v_ref.dtype)


def _qkv_prep(p, rows, cos, sin, qg, kg, n_q, n_kv, tm):
    hd = ATTN_HEAD_DIM
    n_lat_tiles = rows.n_lat // tm
    per_seq = rows.seq // tm
    tab = pl.BlockSpec((tm, hd), lambda i: (jnp.where(i < n_lat_tiles, i % per_seq, per_seq), 0))
    return pl.pallas_call(
        functools.partial(_qkv_kernel, n_q, n_kv),
        grid=(rows.n_all // tm,),
        in_specs=[pl.BlockSpec((tm, p.shape[1]), lambda i: (i, 0)), tab, tab,
                  pl.BlockSpec((1, hd), lambda i: (0, 0)), pl.BlockSpec((1, hd), lambda i: (0, 0))],
        out_specs=[pl.BlockSpec((tm, n_q * hd), lambda i: (i, 0)),
                   pl.BlockSpec((tm, n_kv * hd), lambda i: (i, 0)),
                   pl.BlockSpec((tm, n_kv * hd), lambda i: (i, 0))],
        out_shape=[jax.ShapeDtypeStruct((rows.n_all, n_q * hd), MXU_DTYPE),
                   jax.ShapeDtypeStruct((rows.n_all, n_kv * hd), MXU_DTYPE),
                   jax.ShapeDtypeStruct((rows.n_all, n_kv * hd), MXU_DTYPE)],
        compiler_params=_cparams("parallel"),
        name="qkv_norm_rope",
    )(p, cos, sin, qg, kg)


def _attn_kernel(n_kv_sets, kv_heads, q_ref, *refs):
    kv = refs[:2 * n_kv_sets]
    o_ref, s_scr = refs[2 * n_kv_sets:]
    hd = ATTN_HEAD_DIM
    n_heads = kv_heads * ATTN_GROUP
    bounds = np.cumsum([0] + [kv[2 * i].shape[0] for i in range(n_kv_sets)])

    def scores(g):
        q = q_ref[:, g * hd:(g + 1) * hd]
        c0 = (g // ATTN_GROUP) * hd
        for i in range(n_kv_sets):
            s_scr[g % 2, :, bounds[i]:bounds[i + 1]] = _dot_nt(q, kv[2 * i][:, c0:c0 + hd])

    scores(0)
    for g in range(n_heads):
        if g + 1 < n_heads:
            scores(g + 1)
        s = s_scr[g % 2]
        p = jnp.exp2(s - jnp.max(s, axis=-1, keepdims=True))
        den = jnp.sum(p, axis=-1, keepdims=True)
        pb = p.astype(MXU_DTYPE)
        c0 = (g // ATTN_GROUP) * hd
        acc = functools.reduce(jnp.add, [_dot(pb[:, bounds[i]:bounds[i + 1]], kv[2 * i + 1][:, c0:c0 + hd])
                                         for i in range(n_kv_sets)])
        o_ref[:, g * hd:(g + 1) * hd] = (acc / den).astype(o_ref.dtype)


def _attention(q, k, v, rows, n_kv, tq, ctx_queries):
    hd = ATTN_HEAD_DIM
    kvh = ATTN_KV_PER_STEP if n_kv % ATTN_KV_PER_STEP == 0 else 1
    kw = kvh * hd
    gw = ATTN_GROUP * kw
    ctx_base = rows.n_lat // rows.ctx
    ctx_kv = pl.BlockSpec((rows.ctx, kw), lambda b, h, i: (ctx_base + b, h))
    if ctx_queries:
        n_q, per = rows.batch * rows.ctx, 1
        tq = rows.ctx
        q_spec = pl.BlockSpec((tq, gw), lambda b, h, i: (ctx_base + b, h))
        kv_specs, kv_args, n_keys = [ctx_kv, ctx_kv], [k, v], rows.ctx
    else:
        n_q, per = rows.n_lat, rows.seq // tq
        q_spec = pl.BlockSpec((tq, gw), lambda b, h, i: (b * per + i, h))
        lat_kv = pl.BlockSpec((rows.seq, kw), lambda b, h, i: (b, h))
        kv_specs, kv_args, n_keys = [lat_kv, lat_kv, ctx_kv, ctx_kv], [k, v, k, v], rows.seq + rows.ctx
    return pl.pallas_call(
        functools.partial(_attn_kernel, len(kv_args) // 2, kvh),
        grid=(rows.batch, n_kv // kvh, per),
        in_specs=[q_spec] + kv_specs,
        out_specs=pl.BlockSpec((tq, gw), lambda b, h, i: (b * per + i, h)),
        out_shape=jax.ShapeDtypeStruct((n_q, n_kv * ATTN_GROUP * hd), MXU_DTYPE),
        scratch_shapes=[pltpu.VMEM((2, tq, n_keys), F32)],
        compiler_params=_cparams("parallel", "parallel", "arbitrary"),
        name="gqa_ctx" if ctx_queries else "gqa_latent",
    )(q, *kv_args)


def _rope_perm():
    q = ATTN_HEAD_DIM // 4
    return np.concatenate([np.arange(0, q), np.arange(2 * q, 3 * q), np.arange(q, 2 * q), np.arange(3 * q, 4 * q)])


def _permute_heads(a, n_heads):
    lead = a.shape[:-1]
    return a.reshape(*lead, n_heads, ATTN_HEAD_DIM)[..., _rope_perm()].reshape(*lead, n_heads * ATTN_HEAD_DIM)


def _rope_tables(seq, pad_rows):
    hd = ATTN_HEAD_DIM
    half = hd // 2
    pos = jnp.arange(seq)
    inv_freq = ROPE_THETA ** (-jnp.arange(0, half, 2, dtype=F32) / half)
    ang = jnp.stack([pos // GRID_W, pos % GRID_W], axis=-1).astype(F32)[:, :, None] * inv_freq
    ang = ang.reshape(seq, half)
    cos = jnp.concatenate([jnp.cos(ang), jnp.cos(ang)], axis=-1)
    sin = jnp.concatenate([-jnp.sin(ang), jnp.sin(ang)], axis=-1)
    cos = jnp.concatenate([cos, jnp.ones((pad_rows, hd), F32)], axis=0)
    sin = jnp.concatenate([sin, jnp.zeros((pad_rows, hd), F32)], axis=0)
    return cos, sin


def _final_kernel(x_ref, g_ref, o_ref):
    o_ref[...] = _rms(x_ref[...]) * g_ref[...]


def _final_norm(x, gain, tm):
    n, d = x.shape
    return pl.pallas_call(
        _final_kernel,
        grid=(n // tm,),
        in_specs=[pl.BlockSpec((tm, d), lambda i: (i, 0)), pl.BlockSpec((1, d), lambda i: (0, 0))],
        out_specs=pl.BlockSpec((tm, d), lambda i: (i, 0)),
        out_shape=jax.ShapeDtypeStruct((n, d), F32),
        compiler_params=_cparams("parallel"),
        name="final_norm",
    )(x, gain.reshape(1, d))


def _even_layout(d):
    gla_w = d // 2
    gla_qk = gla_w // 2
    hgrn_w = d // 2
    names = ["gq", "gk", "gv", "gr", "glr", "hq", "hf", "hi", "hg"]
    widths = [gla_qk, gla_qk, gla_w, gla_w, 2 * GLA_GATE_RANK, hgrn_w, 2 * hgrn_w, hgrn_w, hgrn_w]
    src = dict(zip(names, zip((int(o) for o in np.cumsum([0] + widths[:-1])), widths)))
    narrow = {k: src[k] for k in ("gq", "gk", "gv", "gr", "hq", "hi", "hg")}
    wide = {k: src[k] for k in ("glr", "hf")}
    return narrow, wide


def _gather_cols(w, group, pad_to):
    parts, offs, off = [], {}, 0
    for name, (c0, width) in group.items():
        padded = -(-width // pad_to) * pad_to
        parts.append(w[:, c0:c0 + width])
        if padded > width:
            parts.append(jnp.zeros((w.shape[0], padded - width), w.dtype))
        offs[name] = off
        off += padded
    return jnp.concatenate(parts, axis=1), offs


def kernel(x, c, ctx, c_ctx, w_mod, b_mod, ffn_w_gate, ffn_w_up, ffn_w_down, mix_ab_w_in, gla_gate_w2, gla_gate_b,
           gla_norm_g, hgrn_lb_logits, hgrn_norm_g, mix_ab_w_out, attn_w_in, attn_q_norm_g, attn_k_norm_g,
           attn_w_out, final_norm_g):
    batch, seq, d = x.shape
    n_ctx = ctx.shape[1]
    depth = w_mod.shape[0]
    rows = _Rows(batch, seq, n_ctx)
    tm = _row_tile(seq, batch * n_ctx, 512)
    tm_mm = _row_tile(seq, batch * n_ctx, 1024)
    bf = MXU_DTYPE
    wg_all, wu_all, wd_all = ffn_w_gate.astype(bf), ffn_w_up.astype(bf), ffn_w_down.astype(bf)

    gla_w = d // 2
    gla_dv = gla_w // GLA_HEADS
    gla_dk = gla_dv // 2
    hgrn_w = d // 2
    hgrn_heads = hgrn_w // HGRN_EXPAND
    n_q_heads = d // ATTN_HEAD_DIM
    n_kv_heads = n_q_heads // ATTN_GROUP

    cond = jnp.zeros((MOD_ROWS, d), F32).at[:batch].set(c).at[batch].set(c_ctx)
    mod = _modulation(cond, w_mod, b_mod)
    mod = mod.reshape(depth, MOD_ROWS, N_MOD, d).transpose(0, 2, 1, 3).reshape(depth, N_MOD * MOD_ROWS, 1, d)

    narrow, wide = _even_layout(d)
    cos, sin = _rope_tables(seq, tm)

    xs = jnp.concatenate([x.reshape(batch * seq, d), ctx.reshape(batch * n_ctx, d)], axis=0)
    for layer in range(depth):
        last = layer == depth - 1
        j = layer // 2
        m = mod[layer]
        xs = _ffn(xs, rows.n_all, rows, m, (0, 1, 2), wg_all, wu_all, wd_all, (layer, 0), tm_mm)
        if layer % 2 == 0:
            w16, c16 = _gather_cols(mix_ab_w_in[j], narrow, MXU_N)
            w32, c32 = _gather_cols(mix_ab_w_in[j], wide, MXU_N)
            p16 = _proj(xs, rows, m, (3, 4), w16.astype(bf), tm_mm, _col_tile(w16.shape[1]), bf)
            p32 = _proj(xs, rows, m, (3, 4), w32.astype(bf), tm_mm, _col_tile(w32.shape[1]), F32)
            w2p = jnp.zeros((2, LANES, gla_w // 2), F32)
            for direction in range(2):
                r0 = direction * GLA_GATE_RANK
                w2p = w2p.at[direction, r0:r0 + GLA_GATE_RANK].set(gla_gate_w2[j, direction])
            ya_l, ya_c = _gla_mixer(p16, p32, rows, c16, c32, w2p.astype(bf), gla_gate_b[j].reshape(2, 1, -1),
                                    gla_norm_g[j].reshape(1, -1), gla_dk, gla_dv)
            yb_l, yb_c = _hgrn_mixer(p16, p32, rows, c16, c32, hgrn_lb_logits, hgrn_norm_g[j].reshape(1, -1), j,
                                     hgrn_heads, HGRN_EXPAND)
            lat_parts, ctx_parts = [ya_l, yb_l], [ya_c, yb_c]
            w_out = mix_ab_w_out[j].astype(bf)
        else:
            n_qk = (n_q_heads + n_kv_heads) * ATTN_HEAD_DIM
            w_qkv = jnp.concatenate([_permute_heads(attn_w_in[j][:, :n_qk], n_q_heads + n_kv_heads),
                                     attn_w_in[j][:, n_qk:]], axis=1).astype(bf)
            p = _proj(xs, rows, m, (3, 4), w_qkv, tm_mm, _col_tile(w_qkv.shape[1]))
            q, k, v = _qkv_prep(p, rows, cos, sin, _permute_heads(attn_q_norm_g[j].reshape(1, -1), 1),
                                _permute_heads(attn_k_norm_g[j].reshape(1, -1), 1), n_q_heads, n_kv_heads, tm)
            lat_parts = [_attention(q, k, v, rows, n_kv_heads, _row_tile(seq, seq, 256), False)]
            ctx_parts = None if last else [_attention(q, k, v, rows, n_kv_heads, n_ctx, True)]
            w_out = attn_w_out[j].astype(bf)
        if last:
            ctx_parts = None
        xs = _out_proj(xs, rows, m, 5, w_out, lat_parts, ctx_parts, tm)
        n_rows = rows.n_lat if last else rows.n_all
        xs = _ffn(xs, n_rows, rows, m, (6, 7, 8), wg_all, wu_all, wd_all, (layer, 1), tm_mm)
    out = _final_norm(xs, final_norm_g, tm)
    return out.reshape(batch, seq, d)
```

```python
import functools

import jax
import jax.numpy as jnp
import numpy as np
from jax import lax
from jax.experimental import pallas as pl
from jax.experimental.pallas import tpu as pltpu

F32 = jnp.float32
MXU_DTYPE = jnp.bfloat16

EPS = 1e-6
F_FLOOR = 1e-30
N_MOD = 9
GRID_W = 64
GLA_HEADS = 4
GLA_GATE_RANK = 16
GLA_GATE_TAU = 16.0
HGRN_EXPAND = 128
ATTN_HEAD_DIM = 128
ATTN_GROUP = 4
ATTN_KV_PER_STEP = 4
ROPE_THETA = 10000.0

LANES = 128
MXU_N = 256
VMEM_LIMIT_BYTES = 56 * 1024 * 1024

SCAN_CHUNK = 256
SUBLANES = 8
LOG2E = 1.4426950408889634
MOD_ROWS = 16


def _cparams(*sem):
    return pltpu.CompilerParams(dimension_semantics=sem, vmem_limit_bytes=VMEM_LIMIT_BYTES)


def _sigmoid(x):
    return 1.0 / (1.0 + jnp.exp(-x))


def _silu(x):
    return x * _sigmoid(x)


def _rms(x):
    return x * lax.rsqrt(jnp.mean(x * x, axis=-1, keepdims=True) + EPS)


def _dot(a, b):
    return jnp.dot(a, b, preferred_element_type=F32)


def _dot_nt(a, b):
    return lax.dot_general(a, b, (((1,), (1,)), ((), ())), preferred_element_type=F32)


def _dot_tn(a, b):
    return lax.dot_general(a, b, (((0,), (0,)), ((), ())), preferred_element_type=F32)


def _row_tile(n_lat, n_ctx_total, cap):
    t = cap
    while n_lat % t or n_ctx_total % t:
        t //= 2
    return t


def _col_tile(n, cap=1024):
    return max(t for t in range(MXU_N, cap + 1, MXU_N) if n % t == 0)


def _mod_kernel(c_ref, w_ref, b_ref, o_ref):
    s = _silu(c_ref[...]).astype(MXU_DTYPE)
    o_ref[...] = _dot(s, w_ref[...].astype(MXU_DTYPE)) + b_ref[...]


def _modulation(cond, w_mod, b_mod):
    depth, d, n = w_mod.shape
    tn = 1024
    return pl.pallas_call(
        _mod_kernel,
        grid=(depth, n // tn),
        in_specs=[pl.BlockSpec((MOD_ROWS, d), lambda l, j: (0, 0)),
                  pl.BlockSpec((None, d, tn), lambda l, j: (l, 0, j)),
                  pl.BlockSpec((None, 1, tn), lambda l, j: (l, 0, j))],
        out_specs=pl.BlockSpec((None, MOD_ROWS, tn), lambda l, j: (l, 0, j)),
        out_shape=jax.ShapeDtypeStruct((depth, MOD_ROWS, n), F32),
        compiler_params=_cparams("parallel", "parallel"),
        name="modulation",
    )(cond, w_mod, b_mod.reshape(depth, 1, n))


class _Rows:
    def __init__(self, batch, seq, ctx):
        self.batch, self.seq, self.ctx = batch, seq, ctx
        self.n_lat = batch * seq
        self.n_all = self.n_lat + batch * ctx

    def mod_spec(self, tm, width, slot, by_column=False):
        seq, batch = self.seq, self.batch

        def row(i):
            return slot * MOD_ROWS + jnp.minimum((i * tm) // seq, batch)
        if by_column:
            return pl.BlockSpec((None, 1, width), lambda i, j: (row(i), 0, j))
        return pl.BlockSpec((None, 1, width), lambda i, *_: (row(i), 0, 0))


def _ffn_up_kernel(x_ref, sh_ref, sc_ref, wg_ref, wu_ref, o_ref, h_scr):
    @pl.when(pl.program_id(1) == 0)
    def _():
        h = _rms(x_ref[...]) * (1.0 + sc_ref[...]) + sh_ref[...]
        h_scr[...] = h.astype(MXU_DTYPE)

    h = h_scr[...]
    o_ref[...] = (_silu(_dot(h, wg_ref[...])) * _dot(h, wu_ref[...])).astype(o_ref.dtype)


def _ffn_down_kernel(a_ref, wd_ref, x_ref, gt_ref, o_ref):
    o_ref[...] = x_ref[...] + 0.5 * gt_ref[...] * _dot(a_ref[...], wd_ref[...])


def _ffn(x, n_rows, rows, mod, slots, wg, wu, wd, which, tm):
    d = x.shape[1]
    dff = wg.shape[3]
    layer, half = which
    tf = _col_tile(dff, 512)
    act = pl.pallas_call(
        _ffn_up_kernel,
        grid=(n_rows // tm, dff // tf),
        in_specs=[pl.BlockSpec((tm, d), lambda i, j: (i, 0)),
                  rows.mod_spec(tm, d, slots[0]), rows.mod_spec(tm, d, slots[1]),
                  pl.BlockSpec((None, None, d, tf), lambda i, j: (layer, half, 0, j)),
                  pl.BlockSpec((None, None, d, tf), lambda i, j: (layer, half, 0, j))],
        out_specs=pl.BlockSpec((tm, tf), lambda i, j: (i, j)),
        out_shape=jax.ShapeDtypeStruct((n_rows, dff), MXU_DTYPE),
        scratch_shapes=[pltpu.VMEM((tm, d), MXU_DTYPE)],
        compiler_params=_cparams("parallel", "arbitrary"),
        name="ffn_up",
    )(x, mod, mod, wg, wu)
    tn = _col_tile(d, 512)
    return pl.pallas_call(
        _ffn_down_kernel,
        grid=(n_rows // tm, d // tn),
        in_specs=[pl.BlockSpec((tm, dff), lambda i, j: (i, 0)),
                  pl.BlockSpec((None, None, dff, tn), lambda i, j: (layer, half, 0, j)),
                  pl.BlockSpec((tm, tn), lambda i, j: (i, j)),
                  rows.mod_spec(tm, tn, slots[2], by_column=True)],
        out_specs=pl.BlockSpec((tm, tn), lambda i, j: (i, j)),
        out_shape=jax.ShapeDtypeStruct((n_rows, d), F32),
        compiler_params=_cparams("parallel", "arbitrary"),
        name="ffn_down",
    )(act, wd, x, mod)


def _proj_kernel(x_ref, sh_ref, sc_ref, w_ref, o_ref, h_scr):
    @pl.when(pl.program_id(1) == 0)
    def _():
        h = _rms(x_ref[...]) * (1.0 + sc_ref[...]) + sh_ref[...]
        h_scr[...] = h.astype(MXU_DTYPE)

    o_ref[...] = _dot(h_scr[...], w_ref[...]).astype(o_ref.dtype)


def _proj(x, rows, mod, slots, w, tm, tn, out_dtype=F32):
    d = x.shape[1]
    n = w.shape[1]
    return pl.pallas_call(
        _proj_kernel,
        grid=(rows.n_all // tm, n // tn),
        in_specs=[pl.BlockSpec((tm, d), lambda i, j: (i, 0)),
                  rows.mod_spec(tm, d, slots[0]), rows.mod_spec(tm, d, slots[1]),
                  pl.BlockSpec((d, tn), lambda i, j: (0, j))],
        out_specs=pl.BlockSpec((tm, tn), lambda i, j: (i, j)),
        out_shape=jax.ShapeDtypeStruct((rows.n_all, n), out_dtype),
        scratch_shapes=[pltpu.VMEM((tm, d), MXU_DTYPE)],
        compiler_params=_cparams("parallel", "arbitrary"),
        name="mixer_in_proj",
    )(x, mod, mod, w)


def _out_kernel(n_parts, widths, n_lat_tiles, has_ctx, *refs):
    x_ref, gt_ref, w_ref = refs[0], refs[1], refs[2]
    lat = refs[3:3 + n_parts]
    ctx = refs[3 + n_parts:3 + 2 * n_parts] if has_ctx else ()
    o_ref = refs[-1]

    def run(parts):
        acc = None
        off = 0
        for p, wd in zip(parts, widths):
            t = _dot(p[...], w_ref[off:off + wd, :])
            acc = t if acc is None else acc + t
            off += wd
        o_ref[...] = x_ref[...] + gt_ref[...] * acc

    if has_ctx:
        i = pl.program_id(0)
        pl.when(i < n_lat_tiles)(lambda: run(lat))
        pl.when(i >= n_lat_tiles)(lambda: run(ctx))
    else:
        run(lat)


def _out_proj(x, rows, mod, slot, w, lat_parts, ctx_parts, tm):
    d = x.shape[1]
    has_ctx = ctx_parts is not None
    n_rows = rows.n_all if has_ctx else rows.n_lat
    n_lat_tiles = rows.n_lat // tm
    widths = tuple(p.shape[1] for p in lat_parts)
    specs = [pl.BlockSpec((tm, d), lambda i: (i, 0)), rows.mod_spec(tm, d, slot),
             pl.BlockSpec(w.shape, lambda i: (0, 0))]
    specs += [pl.BlockSpec((tm, wd), lambda i: (jnp.minimum(i, n_lat_tiles - 1), 0)) for wd in widths]
    args = [x, mod, w] + list(lat_parts)
    if has_ctx:
        specs += [pl.BlockSpec((tm, wd), lambda i: (jnp.maximum(i - n_lat_tiles, 0), 0)) for wd in widths]
        args += list(ctx_parts)
    return pl.pallas_call(
        functools.partial(_out_kernel, len(widths), widths, n_lat_tiles, has_ctx),
        grid=(n_rows // tm,),
        in_specs=specs,
        out_specs=pl.BlockSpec((tm, d), lambda i: (i, 0)),
        out_shape=jax.ShapeDtypeStruct((n_rows, d), F32),
        compiler_params=_cparams("parallel"),
        name="mixer_out_proj",
    )(*args)


def _scan_tables(chunk):
    c = chunk
    t = np.arange(c)[:, None]
    u = np.arange(c)[None, :]
    tri = (u <= t).astype(np.float32)
    x = t ^ u
    lvl = np.zeros((c, c), np.int32)
    nz = x > 0
    lvl[nz] = int(np.log2(c)) - np.floor(np.log2(x[nz])).astype(np.int32)
    return tri, lvl[:c // 2, :c // 2]


def _level_ref(cum, m):
    c, dk = cum.shape
    if 2 * m >= SUBLANES:
        x = cum.reshape(c // (2 * m), 2 * m, dk)
        return jnp.broadcast_to(x[:, m - 1:m, :], x.shape).reshape(c, dk)
    x = cum.reshape(c // SUBLANES, SUBLANES, dk)
    sub = lax.broadcasted_iota(jnp.int32, x.shape, 1)
    lo, hi = (jnp.broadcast_to(x[:, r:r + 1, :], x.shape) for r in (m - 1, 2 * m + m - 1))
    return jnp.where(sub < 2 * m, lo, hi).reshape(c, dk)


def _chunk_local(q, ks, vb, gs, tri, lvl):
    c, dk = q.shape
    n_levels = int(np.log2(c))
    g_f, g_b = (g * LOG2E for g in gs)
    pieces = []
    for g in (g_f, g_b):
        hi = g.astype(MXU_DTYPE)
        pieces += [hi, (g - hi.astype(F32)).astype(MXU_DTYPE)]
    ex = _dot(tri, jnp.concatenate(pieces, axis=1))
    cum_f = ex[:, :dk] + ex[:, dk:2 * dk]
    cum_b = ex[:, 2 * dk:3 * dk] + ex[:, 3 * dk:]
    tot_f, tot_b = cum_f[c - 1:c], cum_b[c - 1:c]
    a_b = cum_b - g_b
    same_k = ks[0] is ks[1]
    k_sum = ks[0] * 2.0 if same_k else ks[0] + ks[1]
    h = c // 2

    def diag_blocks(zq, zk):
        zq, zk = zq.astype(MXU_DTYPE), zk.astype(MXU_DTYPE)
        return [_dot_nt(zq[:h], zk[:h]), _dot_nt(zq[h:], zk[h:])]

    attn = [jnp.where(lvl == 0, a, 0.0) for a in diag_blocks(q, k_sum)]
    row = lax.broadcasted_iota(jnp.int32, (c, 1), 0)
    for li in range(n_levels):
        m = c >> (li + 1)
        second = (row & m) != 0
        k_m = ks[0] if same_k else jnp.where(second, ks[1], ks[0])
        if m == 1:
            zq = q * jnp.exp2(jnp.where(second, g_f, g_b))
            zk = k_m
        else:
            d_f = cum_f - _level_ref(cum_f, m)
            d_b = a_b - _level_ref(cum_b, m)
            zq = q * jnp.exp2(jnp.where(second, d_f, -d_b))
            zk = k_m * jnp.exp2(jnp.where(second, d_b, -d_f))
        if m == h:
            zq, zk = zq.astype(MXU_DTYPE), zk.astype(MXU_DTYPE)
            lower, upper = _dot_nt(zq[h:], zk[:h]), _dot_nt(zq[:h], zk[h:])
        else:
            attn = [jnp.where(lvl == li + 1, a, old) for a, old in zip(diag_blocks(zq, zk), attn)]
    attn = jnp.concatenate([jnp.concatenate([attn[0], upper], axis=1),
                            jnp.concatenate([lower, attn[1]], axis=1)], axis=0)
    o = _dot(attn.astype(MXU_DTYPE), vb)
    fwd = ((q * jnp.exp2(cum_f)).astype(MXU_DTYPE), (ks[0] * jnp.exp2(tot_f - cum_f)).astype(MXU_DTYPE),
           jnp.exp2(tot_f))
    bwd = ((q * jnp.exp2(tot_b - a_b)).astype(MXU_DTYPE), (ks[1] * jnp.exp2(a_b)).astype(MXU_DTYPE),
           jnp.exp2(tot_b))
    return o, fwd, bwd


def _bidir_scan(segments, chunk, gates, finish, tabs, scr):
    tri, lvl = tabs
    acc, vt_s, qh_s, kh_s, dec_s, st_s = scr
    n_seg = [n_rows // chunk for n_rows, _ in segments]
    n_all = sum(n_seg)

    def rows_of(j):
        return pl.ds(pl.multiple_of(j * chunk, chunk), chunk)

    base = 0
    for (n_rows, seg), n in zip(segments, n_seg):
        def local(ci, _, seg=seg, base=base):
            q, ks, v, gs = gates(seg, rows_of(ci))
            j = base + ci
            vb = v.astype(MXU_DTYPE)
            vt_s[j] = vb.T
            o, fwd, bwd = _chunk_local(q, ks, vb, gs, tri, lvl)
            for d, (qh, kh, dec) in enumerate((fwd, bwd)):
                qh_s[d, rows_of(j), :] = qh
                kh_s[d, rows_of(j), :] = kh
                dec_s[d, pl.ds(pl.multiple_of(j * SUBLANES, SUBLANES), SUBLANES), :] = jnp.broadcast_to(
                    dec, (SUBLANES, dec.shape[1]))
            acc[rows_of(j), :] = o
            return 0
        lax.fori_loop(0, n, local, 0)
        base += n

    st_s[...] = jnp.zeros_like(st_s)
    n_first = n_seg[0]

    def carry(i, _):
        j_b = jnp.where(i < n_first, n_first - 1 - i, n_all + n_first - 1 - i)
        for d, j in enumerate((i, j_b)):
            st = st_s[d]
            acc[rows_of(j), :] += _dot_nt(qh_s[d, rows_of(j), :], st.astype(MXU_DTYPE))
            dec = dec_s[d, pl.ds(pl.multiple_of(j * SUBLANES, SUBLANES), 1), :]
            st_s[d] = st * dec + _dot(vt_s[j], kh_s[d, rows_of(j), :])
        return 0
    lax.fori_loop(0, n_all, carry, 0, unroll=True)

    base = 0
    for (n_rows, seg), n in zip(segments, n_seg):
        def done(ci, _, seg=seg, base=base):
            finish(seg, rows_of(ci), acc[rows_of(base + ci), :])
            return 0
        lax.fori_loop(0, n, done, 0)
        base += n


def _log_sigmoid(x):
    return jnp.minimum(x, 0.0) - jnp.log(1.0 + jnp.exp(-jnp.abs(x)))


def _gla_kernel(chunk, seq, ctx, scale,
                tri, lvl, w2_ref, b_ref, gain_ref,
                q_c, k_c, v_c, gr_c, lr_c, q_l, k_l, v_l, gr_l, lr_l,
                y_c, y_l, *scr):
    ins = ((q_c, k_c, v_c, lr_c, gr_c, y_c), (q_l, k_l, v_l, lr_l, gr_l, y_l))

    def gates(seg, sl):
        q_r, k_r, v_r, lr_r = ins[seg][:4]
        lr = lr_r[sl, :].astype(MXU_DTYPE)
        gs = tuple(_log_sigmoid(_dot(lr, w2_ref[d]) + b_ref[d]) * (1.0 / GLA_GATE_TAU) for d in range(2))
        k = k_r[sl, :].astype(F32)
        return q_r[sl, :].astype(F32) * scale, (k, k), v_r[sl, :], gs

    def finish(seg, sl, o):
        gr_r, y_r = ins[seg][4:]
        y_r[sl, :] = (_rms(o) * gain_ref[...] * _silu(gr_r[sl, :].astype(F32))).astype(y_r.dtype)

    _bidir_scan([(ctx, 0), (seq, 1)], chunk, gates, finish, (tri[...], lvl[...]), scr)


def _hgrn_kernel(chunk, seq, ctx, layer_j,
                 tri, lvl, lbl_ref, gain_ref,
                 q_c, i_c, hg_c, ff_c, fb_c, q_l, i_l, hg_l, ff_l, fb_l,
                 y_c, y_l, *scr):
    ins = ((q_c, (ff_c, fb_c), i_c, hg_c, y_c), (q_l, (ff_l, fb_l), i_l, hg_l, y_l))
    lbs = []
    for direction in range(2):
        logits = lbl_ref[direction]
        pe = jnp.exp(logits - jnp.max(logits, axis=0, keepdims=True))
        p = pe / jnp.sum(pe, axis=0, keepdims=True)
        lbs.append(jnp.sum(p[0:layer_j + 1], axis=0, keepdims=True) - p[0:1])

    def gates(seg, sl):
        q_r, f_rs, i_r = ins[seg][:3]
        ks, gs = [], []
        for d in range(2):
            lb = lbs[d]
            x = f_rs[d][sl, :]
            en = jnp.exp(-jnp.abs(x))
            inv = 1.0 / (1.0 + en)
            pos = x >= 0.0
            sig_p = jnp.where(pos, inv, en * inv)
            sig_n = jnp.where(pos, en * inv, inv)
            gs.append(jnp.log(jnp.maximum(lb + (1.0 - lb) * sig_p, F_FLOOR)))
            ks.append((1.0 - lb) * sig_n)
        return _silu(q_r[sl, :].astype(F32)), ks, i_r[sl, :], gs

    def finish(seg, sl, o):
        hg_r, y_r = ins[seg][3:]
        y_r[sl, :] = (_rms(o) * gain_ref[...] * _silu(hg_r[sl, :].astype(F32))).astype(y_r.dtype)

    _bidir_scan([(ctx, 0), (seq, 1)], chunk, gates, finish, (tri[...], lvl[...]), scr)


def _scan_scratch(chunk, n_rows, dk, dv):
    n_chunks = n_rows // chunk
    return [pltpu.VMEM((n_rows, dv), F32), pltpu.VMEM((n_chunks, dv, chunk), MXU_DTYPE),
            pltpu.VMEM((2, n_rows, dk), MXU_DTYPE), pltpu.VMEM((2, n_rows, dk), MXU_DTYPE),
            pltpu.VMEM((2, n_chunks * SUBLANES, dk), F32), pltpu.VMEM((2, dv, dk), F32)]


def _scan_chunk(rows):
    return _row_tile(rows.seq, rows.ctx, SCAN_CHUNK)


def _scan_table_args(chunk):
    tri, lvl = _scan_tables(chunk)
    args = [jnp.asarray(tri, MXU_DTYPE), jnp.asarray(lvl)]
    specs = [pl.BlockSpec(a.shape, lambda b, h: (0, 0)) for a in args]
    return args, specs


def _seg_inputs(rows, cols):
    ctx_base = rows.n_lat // rows.ctx
    specs, arrays = [], []
    for n_rows, base in ((rows.ctx, ctx_base), (rows.seq, 0)):
        for arr, w, c0, per_head in cols:
            specs.append(pl.BlockSpec(
                (n_rows, w), lambda b, h, base=base, cb=c0 // w, ph=int(per_head): (base + b, cb + ph * h)))
            arrays.append(arr)
    return specs, arrays


def _scan_call(body, name, rows, n_heads, dk, dv, chunk, param_specs, params, seg_cols):
    targs, tspecs = _scan_table_args(chunk)
    seg_specs, seg_arrays = _seg_inputs(rows, seg_cols)
    width = n_heads * dv
    y_c, y_l = pl.pallas_call(
        body,
        grid=(rows.batch, n_heads),
        in_specs=tspecs + param_specs + seg_specs,
        out_specs=[pl.BlockSpec((rows.ctx, dv), lambda b, h: (b, h)),
                   pl.BlockSpec((rows.seq, dv), lambda b, h: (b, h))],
        out_shape=[jax.ShapeDtypeStruct((rows.batch * rows.ctx, width), MXU_DTYPE),
                   jax.ShapeDtypeStruct((rows.n_lat, width), MXU_DTYPE)],
        scratch_shapes=_scan_scratch(chunk, rows.ctx + rows.seq, dk, dv),
        compiler_params=_cparams("parallel", "parallel"),
        name=name,
    )(*targs, *params, *seg_arrays)
    return y_l, y_c


def _gla_mixer(p16, p32, rows, c16, c32, w2p, b2, gain, dk, dv):
    chunk = _scan_chunk(rows)
    u = LANES
    param_specs = [pl.BlockSpec((2, u, dk), lambda b, h: (0, 0, h)),
                   pl.BlockSpec((2, 1, dk), lambda b, h: (0, 0, h)),
                   pl.BlockSpec((1, dv), lambda b, h: (0, 0))]
    seg_cols = [(p16, dk, c16["gq"], True), (p16, dk, c16["gk"], True), (p16, dv, c16["gv"], True),
                (p16, dv, c16["gr"], True), (p32, u, c32["glr"], False)]
    return _scan_call(functools.partial(_gla_kernel, chunk, rows.seq, rows.ctx, dk ** -0.5), "gla_scan",
                      rows, GLA_HEADS, dk, dv, chunk, param_specs, [w2p, b2, gain], seg_cols)


def _hgrn_mixer(p16, p32, rows, c16, c32, lb_logits, gain, layer_j, n_heads, dk):
    chunk = _scan_chunk(rows)
    n_even = lb_logits.shape[1]
    param_specs = [pl.BlockSpec((2, n_even, dk), lambda b, h: (0, 0, h)),
                   pl.BlockSpec((1, dk), lambda b, h: (0, 0))]
    seg_cols = [(p16, dk, c16["hq"], True), (p16, dk, c16["hi"], True), (p16, dk, c16["hg"], True),
                (p32, dk, c32["hf"], True), (p32, dk, c32["hf"] + n_heads * dk, True)]
    return _scan_call(functools.partial(_hgrn_kernel, chunk, rows.seq, rows.ctx, layer_j), "hgrn_scan",
                      rows, n_heads, dk, dk, chunk, param_specs, [lb_logits, gain], seg_cols)


def _qkv_kernel(n_q, n_kv, p_ref, cos_ref, sin_ref, qg_ref, kg_ref, q_ref, k_ref, v_ref):
    hd = ATTN_HEAD_DIM
    cos, sin = cos_ref[...], sin_ref[...]

    def rope(a):
        return a * cos + pltpu.roll(a, hd // 2, axis=1) * sin

    q_gain = qg_ref[...] * (hd ** -0.5 * LOG2E)
    for h in range(n_q):
        q_ref[:, h * hd:(h + 1) * hd] = rope(_rms(p_ref[:, h * hd:(h + 1) * hd]) * q_gain).astype(q_ref.dtype)
    for h in range(n_kv):
        a = _rms(p_ref[:, (n_q + h) * hd:(n_q + h + 1) * hd]) * kg_ref[...]
        k_ref[:, h * hd:(h + 1) * hd] = rope(a).astype(k_ref.dtype)
    for h in range(n_kv):
        v_ref[:, 2 * h * hd:(2 * h + 1) * hd] = p_ref[:, (n_q + n_kv + h) * hd:(n_q + n_kv + h + 1) * hd].astype(
            v_ref.dtype)
        v_ref[:, (2 * h + 1) * hd:(2 * h + 2) * hd] = jnp.ones((v_ref.shape[0], hd), v_ref.dtype)


def _qkv_prep(p, rows, cos, sin, qg, kg, n_q, n_kv, tm):
    hd = ATTN_HEAD_DIM
    n_lat_tiles = rows.n_lat // tm
    per_seq = rows.seq // tm
    tab = pl.BlockSpec((tm, hd), lambda i: (jnp.where(i < n_lat_tiles, i % per_seq, per_seq), 0))
    return pl.pallas_call(
        functools.partial(_qkv_kernel, n_q, n_kv),
        grid=(rows.n_all // tm,),
        in_specs=[pl.BlockSpec((tm, p.shape[1]), lambda i: (i, 0)), tab, tab,
                  pl.BlockSpec((1, hd), lambda i: (0, 0)), pl.BlockSpec((1, hd), lambda i: (0, 0))],
        out_specs=[pl.BlockSpec((tm, n_q * hd), lambda i: (i, 0)),
                   pl.BlockSpec((tm, n_kv * hd), lambda i: (i, 0)),
                   pl.BlockSpec((tm, 2 * n_kv * hd), lambda i: (i, 0))],
        out_shape=[jax.ShapeDtypeStruct((rows.n_all, n_q * hd), MXU_DTYPE),
                   jax.ShapeDtypeStruct((rows.n_all, n_kv * hd), MXU_DTYPE),
                   jax.ShapeDtypeStruct((rows.n_all, 2 * n_kv * hd), MXU_DTYPE)],
        compiler_params=_cparams("parallel"),
        name="qkv_norm_rope",
    )(p, cos, sin, qg, kg)


def _attn_kernel(n_kv_sets, kv_heads, q_ref, *refs):
    kv = refs[:2 * n_kv_sets]
    o_ref, s_scr = refs[2 * n_kv_sets:]
    hd = ATTN_HEAD_DIM
    n_heads = kv_heads * ATTN_GROUP
    bounds = np.cumsum([0] + [kv[2 * i].shape[0] for i in range(n_kv_sets)])

    def scores(g):
        q = q_ref[:, g * hd:(g + 1) * hd]
        c0 = (g // ATTN_GROUP) * hd
        for i in range(n_kv_sets):
            s_scr[g % 2, :, bounds[i]:bounds[i + 1]] = _dot_nt(q, kv[2 * i][:, c0:c0 + hd])

    scores(0)
    for g in range(n_heads):
        if g + 1 < n_heads:
            scores(g + 1)
        s = s_scr[g % 2]
        p = jnp.exp2(s - jnp.max(s, axis=-1, keepdims=True))
        pb = p.astype(MXU_DTYPE)
        c0 = (g // ATTN_GROUP) * 2 * hd
        acc = functools.reduce(jnp.add, [_dot(pb[:, bounds[i]:bounds[i + 1]], kv[2 * i + 1][:, c0:c0 + 2 * hd])
                                         for i in range(n_kv_sets)])
        o_ref[:, g * hd:(g + 1) * hd] = (acc[:, :hd] / acc[:, hd:]).astype(o_ref.dtype)


def _attention(q, k, v, rows, n_kv, tq, ctx_queries):
    hd = ATTN_HEAD_DIM
    kvh = ATTN_KV_PER_STEP if n_kv % ATTN_KV_PER_STEP == 0 else 1
    kw = kvh * hd
    gw = ATTN_GROUP * kw
    ctx_base = rows.n_lat // rows.ctx
    ctx_kv = [pl.BlockSpec((rows.ctx, w), lambda b, h, i: (ctx_base + b, h)) for w in (kw, 2 * kw)]
    if ctx_queries:
        n_q, per = rows.batch * rows.ctx, 1
        tq = rows.ctx
        q_spec = pl.BlockSpec((tq, gw), lambda b, h, i: (ctx_base + b, h))
        kv_specs, kv_args, n_keys = ctx_kv, [k, v], rows.ctx
    else:
        n_q, per = rows.n_lat, rows.seq // tq
        q_spec = pl.BlockSpec((tq, gw), lambda b, h, i: (b * per + i, h))
        lat_kv = [pl.BlockSpec((rows.seq, w), lambda b, h, i: (b, h)) for w in (kw, 2 * kw)]
        kv_specs, kv_args, n_keys = lat_kv + ctx_kv, [k, v, k, v], rows.seq + rows.ctx
    return pl.pallas_call(
        functools.partial(_attn_kernel, len(kv_args) // 2, kvh),
        grid=(rows.batch, n_kv // kvh, per),
        in_specs=[q_spec] + kv_specs,
        out_specs=pl.BlockSpec((tq, gw), lambda b, h, i: (b * per + i, h)),
        out_shape=jax.ShapeDtypeStruct((n_q, n_kv * ATTN_GROUP * hd), MXU_DTYPE),
        scratch_shapes=[pltpu.VMEM((2, tq, n_keys), F32)],
        compiler_params=_cparams("parallel", "parallel", "arbitrary"),
        name="gqa_ctx" if ctx_queries else "gqa_latent",
    )(q, *kv_args)


def _rope_perm():
    q = ATTN_HEAD_DIM // 4
    return np.concatenate([np.arange(0, q), np.arange(2 * q, 3 * q), np.arange(q, 2 * q), np.arange(3 * q, 4 * q)])


def _permute_heads(a, n_heads):
    lead = a.shape[:-1]
    return a.reshape(*lead, n_heads, ATTN_HEAD_DIM)[..., _rope_perm()].reshape(*lead, n_heads * ATTN_HEAD_DIM)


def _rope_tables(seq, pad_rows):
    hd = ATTN_HEAD_DIM
    half = hd // 2
    pos = jnp.arange(seq)
    inv_freq = ROPE_THETA ** (-jnp.arange(0, half, 2, dtype=F32) / half)
    ang = jnp.stack([pos // GRID_W, pos % GRID_W], axis=-1).astype(F32)[:, :, None] * inv_freq
    ang = ang.reshape(seq, half)
    cos = jnp.concatenate([jnp.cos(ang), jnp.cos(ang)], axis=-1)
    sin = jnp.concatenate([-jnp.sin(ang), jnp.sin(ang)], axis=-1)
    cos = jnp.concatenate([cos, jnp.ones((pad_rows, hd), F32)], axis=0)
    sin = jnp.concatenate([sin, jnp.zeros((pad_rows, hd), F32)], axis=0)
    return cos, sin


def _final_kernel(x_ref, g_ref, o_ref):
    o_ref[...] = _rms(x_ref[...]) * g_ref[...]


def _final_norm(x, gain, tm):
    n, d = x.shape
    return pl.pallas_call(
        _final_kernel,
        grid=(n // tm,),
        in_specs=[pl.BlockSpec((tm, d), lambda i: (i, 0)), pl.BlockSpec((1, d), lambda i: (0, 0))],
        out_specs=pl.BlockSpec((tm, d), lambda i: (i, 0)),
        out_shape=jax.ShapeDtypeStruct((n, d), F32),
        compiler_params=_cparams("parallel"),
        name="final_norm",
    )(x, gain.reshape(1, d))


def _even_layout(d):
    gla_w = d // 2
    gla_qk = gla_w // 2
    hgrn_w = d // 2
    names = ["gq", "gk", "gv", "gr", "glr", "hq", "hf", "hi", "hg"]
    widths = [gla_qk, gla_qk, gla_w, gla_w, 2 * GLA_GATE_RANK, hgrn_w, 2 * hgrn_w, hgrn_w, hgrn_w]
    src = dict(zip(names, zip((int(o) for o in np.cumsum([0] + widths[:-1])), widths)))
    narrow = {k: src[k] for k in ("gq", "gk", "gv", "gr", "hq", "hi", "hg")}
    wide = {k: src[k] for k in ("glr", "hf")}
    return narrow, wide


def _gather_cols(w, group, pad_to):
    parts, offs, off = [], {}, 0
    for name, (c0, width) in group.items():
        padded = -(-width // pad_to) * pad_to
        parts.append(w[:, c0:c0 + width])
        if padded > width:
            parts.append(jnp.zeros((w.shape[0], padded - width), w.dtype))
        offs[name] = off
        off += padded
    return jnp.concatenate(parts, axis=1), offs


def kernel(x, c, ctx, c_ctx, w_mod, b_mod, ffn_w_gate, ffn_w_up, ffn_w_down, mix_ab_w_in, gla_gate_w2, gla_gate_b,
           gla_norm_g, hgrn_lb_logits, hgrn_norm_g, mix_ab_w_out, attn_w_in, attn_q_norm_g, attn_k_norm_g,
           attn_w_out, final_norm_g):
    batch, seq, d = x.shape
    n_ctx = ctx.shape[1]
    depth = w_mod.shape[0]
    rows = _Rows(batch, seq, n_ctx)
    tm = _row_tile(seq, batch * n_ctx, 512)
    tm_mm = _row_tile(seq, batch * n_ctx, 1024)
    bf = MXU_DTYPE
    wg_all, wu_all, wd_all = ffn_w_gate.astype(bf), ffn_w_up.astype(bf), ffn_w_down.astype(bf)

    gla_w = d // 2
    gla_dv = gla_w // GLA_HEADS
    gla_dk = gla_dv // 2
    hgrn_w = d // 2
    hgrn_heads = hgrn_w // HGRN_EXPAND
    n_q_heads = d // ATTN_HEAD_DIM
    n_kv_heads = n_q_heads // ATTN_GROUP

    cond = jnp.zeros((MOD_ROWS, d), F32).at[:batch].set(c).at[batch].set(c_ctx)
    mod = _modulation(cond, w_mod, b_mod)
    mod = mod.reshape(depth, MOD_ROWS, N_MOD, d).transpose(0, 2, 1, 3).reshape(depth, N_MOD * MOD_ROWS, 1, d)

    narrow, wide = _even_layout(d)
    cos, sin = _rope_tables(seq, tm)

    xs = jnp.concatenate([x.reshape(batch * seq, d), ctx.reshape(batch * n_ctx, d)], axis=0)
    for layer in range(depth):
        last = layer == depth - 1
        j = layer // 2
        m = mod[layer]
        xs = _ffn(xs, rows.n_all, rows, m, (0, 1, 2), wg_all, wu_all, wd_all, (layer, 0), tm_mm)
        if layer % 2 == 0:
            w16, c16 = _gather_cols(mix_ab_w_in[j], narrow, MXU_N)
            w32, c32 = _gather_cols(mix_ab_w_in[j], wide, MXU_N)
            p16 = _proj(xs, rows, m, (3, 4), w16.astype(bf), tm_mm, _col_tile(w16.shape[1]), bf)
            p32 = _proj(xs, rows, m, (3, 4), w32.astype(bf), tm_mm, _col_tile(w32.shape[1]), F32)
            w2p = jnp.zeros((2, LANES, gla_w // 2), F32)
            for direction in range(2):
                r0 = direction * GLA_GATE_RANK
                w2p = w2p.at[direction, r0:r0 + GLA_GATE_RANK].set(gla_gate_w2[j, direction])
            ya_l, ya_c = _gla_mixer(p16, p32, rows, c16, c32, w2p.astype(bf), gla_gate_b[j].reshape(2, 1, -1),
                                    gla_norm_g[j].reshape(1, -1), gla_dk, gla_dv)
            yb_l, yb_c = _hgrn_mixer(p16, p32, rows, c16, c32, hgrn_lb_logits, hgrn_norm_g[j].reshape(1, -1), j,
                                     hgrn_heads, HGRN_EXPAND)
            lat_parts, ctx_parts = [ya_l, yb_l], [ya_c, yb_c]
            w_out = mix_ab_w_out[j].astype(bf)
        else:
            n_qk = (n_q_heads + n_kv_heads) * ATTN_HEAD_DIM
            w_qkv = jnp.concatenate([_permute_heads(attn_w_in[j][:, :n_qk], n_q_heads + n_kv_heads),
                                     attn_w_in[j][:, n_qk:]], axis=1).astype(bf)
            p = _proj(xs, rows, m, (3, 4), w_qkv, tm_mm, _col_tile(w_qkv.shape[1]))
            q, k, v = _qkv_prep(p, rows, cos, sin, _permute_heads(attn_q_norm_g[j].reshape(1, -1), 1),
                                _permute_heads(attn_k_norm_g[j].reshape(1, -1), 1), n_q_heads, n_kv_heads, tm)
            lat_parts = [_attention(q, k, v, rows, n_kv_heads, _row_tile(seq, seq, 256), False)]
            ctx_parts = None if last else [_attention(q, k, v, rows, n_kv_heads, n_ctx, True)]
            w_out = attn_w_out[j].astype(bf)
        if last:
            ctx_parts = None
        xs = _out_proj(xs, rows, m, 5, w_out, lat_parts, ctx_parts, tm)
        n_rows = rows.n_lat if last else rows.n_all
        xs = _ffn(xs, n_rows, rows, m, (6, 7, 8), wg_all, wu_all, wd_all, (layer, 1), tm_mm)
    out = _final_norm(xs, final_norm_g, tm)
    return out.reshape(batch, seq, d)
```
